```python
import math
import jax, jax.numpy as jnp
from jax import lax
import numpy as np

D_MODEL = 1024
BATCH = 1
SEQ = 16384
DEPTH = 4

N_MIXERS = 3
FOX_HEADS = 16
FOX_HEAD_DIM = D_MODEL // FOX_HEADS
Q_BLOCK = 128
S5_GROUP = 16
S5_GROUPS = D_MODEL // S5_GROUP
S5_STATE = 64
SCAN_CHUNK = 128
POOL_WINDOWS = (2, 4, 8, 16)
POOL_GROUP = D_MODEL // len(POOL_WINDOWS)
D_FF = ((8 * D_MODEL + 3 * 256 - 1) // (3 * 256)) * 256
DN_ALPHA = (2.0 * DEPTH) ** 0.25
DN_BETA = (8.0 * DEPTH) ** -0.25
LN_EPS = 1e-5
N_FOX = (DEPTH + 2) // 3
N_S5 = (DEPTH + 1) // 3
N_POOL = DEPTH // 3

kernel_name = "fox_s5_pool_interleaved_deepnorm"


def layer_norm(h, g, b):
    h32 = h.astype(jnp.float32)
    mu = jnp.mean(h32, axis=-1, keepdims=True)
    var = jnp.mean(jnp.square(h32 - mu), axis=-1, keepdims=True)
    return ((h32 - mu) * lax.rsqrt(var + LN_EPS) * g + b).astype(h.dtype)


def fox_mixer(x, w_qkv, w_f, b_f, w_o):
    bsz, s, _ = x.shape
    q, k, v = jnp.split(x @ w_qkv, 3, axis=-1)
    q = q.reshape(bsz, s, FOX_HEADS, FOX_HEAD_DIM)
    k = k.reshape(bsz, s, FOX_HEADS, FOX_HEAD_DIM)
    v = v.reshape(bsz, s, FOX_HEADS, FOX_HEAD_DIM)
    log_f = jax.nn.log_sigmoid((x @ w_f + b_f).astype(jnp.float32))
    c = jnp.cumsum(log_f, axis=1)
    c_k = jnp.transpose(c, (0, 2, 1))
    nb = s // Q_BLOCK
    q_blocks = q.reshape(bsz, nb, Q_BLOCK, FOX_HEADS, FOX_HEAD_DIM).transpose(1, 0, 2, 3, 4)
    c_blocks = c.reshape(bsz, nb, Q_BLOCK, FOX_HEADS).transpose(1, 0, 3, 2)
    k_pos = jnp.arange(s)
    scale = FOX_HEAD_DIM ** -0.5

    def block(args):
        qb, cb, i = args
        logits = jnp.einsum('bqhd,bkhd->bhqk', qb, k).astype(jnp.float32) * scale
        logits = logits + cb[..., :, None] - c_k[..., None, :]
        q_pos = i * Q_BLOCK + jnp.arange(Q_BLOCK)
        mask = k_pos[None, :] <= q_pos[:, None]
        logits = jnp.where(mask, logits, -jnp.inf)
        p = jax.nn.softmax(logits, axis=-1).astype(v.dtype)
        return jnp.einsum('bhqk,bkhd->bqhd', p, v)

    o = lax.map(block, (q_blocks, c_blocks, jnp.arange(nb)))
    o = o.transpose(1, 0, 2, 3, 4).reshape(bsz, s, D_MODEL)
    return o @ w_o


def s5_mixer(x, w_in, a_re, a_im, log_dt, b_re, b_im, c_re, c_im, d_skip, w_glu, b_glu):
    bsz, s, _ = x.shape
    f32 = jnp.float32
    u = x @ w_in
    u32 = u.astype(f32).reshape(bsz, s, S5_GROUPS, S5_GROUP)
    dt = jnp.exp(log_dt.astype(f32))[:, None]
    ar, ai = a_re.astype(f32), a_im.astype(f32)
    mag = jnp.exp(ar * dt)
    lr, li = mag * jnp.cos(ai * dt), mag * jnp.sin(ai * dt)
    den = ar * ar + ai * ai
    zr = ((lr - 1.0) * ar + li * ai) / den
    zi = (li * ar - (lr - 1.0) * ai) / den
    br, bi = b_re.astype(f32), b_im.astype(f32)
    bbr = zr[..., None] * br - zi[..., None] * bi
    bbi = zr[..., None] * bi + zi[..., None] * br
    cr, ci = c_re.astype(f32), c_im.astype(f32)
    nc = s // SCAN_CHUNK
    u_chunks = u32.reshape(bsz, nc, SCAN_CHUNK, S5_GROUPS, S5_GROUP).transpose(1, 0, 2, 3, 4)

    def combine(e1, e2):
        a1r, a1i, b1r, b1i = e1
        a2r, a2i, b2r, b2i = e2
        return (a2r * a1r - a2i * a1i,
                a2r * a1i + a2i * a1r,
                a2r * b1r - a2i * b1i + b2r,
                a2r * b1i + a2i * b1r + b2i)

    def chunk_step(carry, uc):
        h0r, h0i = carry
        bur = jnp.einsum('gpc,blgc->blgp', bbr, uc)
        bui = jnp.einsum('gpc,blgc->blgp', bbi, uc)
        lam_r = jnp.broadcast_to(lr, bur.shape)
        lam_i = jnp.broadcast_to(li, bur.shape)
        pr, pi, hr, hi = lax.associative_scan(combine, (lam_r, lam_i, bur, bui), axis=1)
        hr, hi = (hr + pr * h0r[:, None] - pi * h0i[:, None],
                  hi + pr * h0i[:, None] + pi * h0r[:, None])
        y = jnp.einsum('gcp,blgp->blgc', cr, hr) - jnp.einsum('gcp,blgp->blgc', ci, hi)
        return (hr[:, -1], hi[:, -1]), y

    init = (jnp.zeros((bsz, S5_GROUPS, S5_STATE), f32),
            jnp.zeros((bsz, S5_GROUPS, S5_STATE), f32))
    _, y = lax.scan(chunk_step, init, u_chunks)
    y = y.transpose(1, 0, 2, 3, 4).reshape(bsz, s, D_MODEL)
    y = y + d_skip.astype(f32) * u32.reshape(bsz, s, D_MODEL)
    y = jax.nn.gelu(y).astype(x.dtype)
    val, gate = jnp.split(y @ w_glu + b_glu, 2, axis=-1)
    return val * jax.nn.sigmoid(gate)


def pool_mixer(x, w_in, w_grp, scale, w_out):
    bsz, s, _ = x.shape
    u32 = (x @ w_in).astype(jnp.float32)
    cs = jnp.cumsum(u32, axis=1)
    pos1 = jnp.arange(1, s + 1)
    outs = []
    for gi, w in enumerate(POOL_WINDOWS):
        sl = slice(gi * POOL_GROUP, (gi + 1) * POOL_GROUP)
        csg = cs[..., sl]
        lag = jnp.pad(csg, ((0, 0), (w, 0), (0, 0)))[:, :s]
        cnt = jnp.minimum(pos1, w).astype(jnp.float32)[None, :, None]
        outs.append((csg - lag) / cnt - u32[..., sl])
    m = jnp.stack(outs, axis=2).astype(x.dtype)
    m = jnp.einsum('bsgc,gcd->bsgd', m, w_grp).reshape(bsz, s, D_MODEL)
    return (m * scale) @ w_out


def swiglu(x, w1, w3, w2):
    return (jax.nn.silu(x @ w1) * (x @ w3)) @ w2


def setup_inputs(seed: int = 0) -> dict:
    key = jax.random.key(seed)
    ks = jax.random.split(key, 32)
    f32 = jnp.float32

    def nrm(k, shape, sc):
        return jax.random.normal(k, shape, f32) * sc

    d = D_MODEL
    a_im_base = math.pi * jnp.arange(S5_STATE, dtype=f32)
    return {
        "x": nrm(ks[0], (BATCH, SEQ, d), 1.0),
        "ln_g": 1.0 + nrm(ks[1], (DEPTH, 2, d), 0.02),
        "ln_b": nrm(ks[2], (DEPTH, 2, d), 0.02),
        "fox_w_qkv": nrm(ks[3], (N_FOX, d, 3 * d), d ** -0.5),
        "fox_w_f": nrm(ks[4], (N_FOX, d, FOX_HEADS), 0.1 * d ** -0.5),
        "fox_b_f": jax.random.uniform(ks[5], (N_FOX, FOX_HEADS), f32, 1.0, 6.0),
        "fox_w_o": nrm(ks[6], (N_FOX, d, d), DN_BETA * d ** -0.5),
        "s5_w_in": nrm(ks[7], (N_S5, d, d), d ** -0.5),
        "s5_a_re": -0.5 + nrm(ks[8], (N_S5, S5_GROUPS, S5_STATE), 0.01),
        "s5_a_im": a_im_base + nrm(ks[9], (N_S5, S5_GROUPS, S5_STATE), 0.01),
        "s5_log_dt": jax.random.uniform(ks[10], (N_S5, S5_GROUPS), f32, math.log(1e-3), math.log(1e-1)),
        "s5_b_re": nrm(ks[11], (N_S5, S5_GROUPS, S5_STATE, S5_GROUP), (2 * S5_GROUP) ** -0.5),
        "s5_b_im": nrm(ks[12], (N_S5, S5_GROUPS, S5_STATE, S5_GROUP), (2 * S5_GROUP) ** -0.5),
        "s5_c_re": nrm(ks[13], (N_S5, S5_GROUPS, S5_GROUP, S5_STATE), S5_STATE ** -0.5),
        "s5_c_im": nrm(ks[14], (N_S5, S5_GROUPS, S5_GROUP, S5_STATE), S5_STATE ** -0.5),
        "s5_d": nrm(ks[15], (N_S5, d), 1.0),
        "s5_w_glu": nrm(ks[16], (N_S5, d, 2 * d), DN_BETA * d ** -0.5),
        "s5_b_glu": nrm(ks[17], (N_S5, 2 * d), 0.02),
        "pool_w_in": nrm(ks[18], (N_POOL, d, d), d ** -0.5),
        "pool_w_grp": nrm(ks[19], (N_POOL, len(POOL_WINDOWS), POOL_GROUP, POOL_GROUP), POOL_GROUP ** -0.5),
        "pool_scale": 1.0 + nrm(ks[20], (N_POOL, d), 0.02),
        "pool_w_out": nrm(ks[21], (N_POOL, d, d), DN_BETA * d ** -0.5),
        "ffn_w1": nrm(ks[22], (DEPTH, d, D_FF), d ** -0.5),
        "ffn_w3": nrm(ks[23], (DEPTH, d, D_FF), d ** -0.5),
        "ffn_w2": nrm(ks[24], (DEPTH, D_FF, d), DN_BETA * D_FF ** -0.5),
    }


def reference(x, ln_g, ln_b,
              fox_w_qkv, fox_w_f, fox_b_f, fox_w_o,
              s5_w_in, s5_a_re, s5_a_im, s5_log_dt, s5_b_re, s5_b_im, s5_c_re, s5_c_im,
              s5_d, s5_w_glu, s5_b_glu,
              pool_w_in, pool_w_grp, pool_scale, pool_w_out,
              ffn_w1, ffn_w3, ffn_w2):
    h = x
    for i in range(DEPTH):
        kind, j = i % N_MIXERS, i // N_MIXERS
        if kind == 0:
            m = fox_mixer(h, fox_w_qkv[j], fox_w_f[j], fox_b_f[j], fox_w_o[j])
        elif kind == 1:
            m = s5_mixer(h, s5_w_in[j], s5_a_re[j], s5_a_im[j], s5_log_dt[j],
                         s5_b_re[j], s5_b_im[j], s5_c_re[j], s5_c_im[j],
                         s5_d[j], s5_w_glu[j], s5_b_glu[j])
        else:
            m = pool_mixer(h, pool_w_in[j], pool_w_grp[j], pool_scale[j], pool_w_out[j])
        h = layer_norm(DN_ALPHA * h + m, ln_g[i, 0], ln_b[i, 0])
        h = layer_norm(DN_ALPHA * h + swiglu(h, ffn_w1[i], ffn_w3[i], ffn_w2[i]), ln_g[i, 1], ln_b[i, 1])
    return h
```

```python
import functools
import math

import jax
import jax.numpy as jnp
from jax import lax
from jax.experimental import pallas as pl
from jax.experimental.pallas import tpu as pltpu

F32 = jnp.float32
BF16 = jnp.bfloat16

D_MODEL = 1024
SEQ = 16384
DEPTH = 4
N_MIXERS = 3
FOX_HEADS = 16
FOX_HEAD_DIM = D_MODEL // FOX_HEADS
S5_GROUP = 16
S5_GROUPS = D_MODEL // S5_GROUP
S5_STATE = 64
POOL_WINDOWS = (2, 4, 8, 16)
POOL_GROUP = D_MODEL // len(POOL_WINDOWS)
D_FF = 2816
DN_ALPHA = (2.0 * DEPTH) ** 0.25
LN_EPS = 1e-5

LANES = 128
VMEM_LIMIT = 56 * 1024 * 1024

ROW_BLOCK = 512
FF_CHUNK = 256
ATT_BLOCK = 512
HEAD_SLOT = 2 * FOX_HEAD_DIM
S5_CHUNK = 128
S5_GB = 8
S5_GB_CH = S5_GB * S5_GROUP
S5_GB_ST = S5_GB * S5_STATE
POOL_HALO = 16
NEG_BIG = -1e30


def _resident(shape):
    nd = len(shape)
    return pl.BlockSpec(shape, lambda *_: (0,) * nd, pipeline_mode=pl.Buffered(1))


def _params(*sem):
    return pltpu.CompilerParams(dimension_semantics=sem, vmem_limit_bytes=VMEM_LIMIT)


def _dot(a, b):
    return jnp.dot(a, b, preferred_element_type=F32)


def _layer_norm(z, g, b):
    mu = jnp.mean(z, axis=-1, keepdims=True)
    zc = z - mu
    var = jnp.mean(zc * zc, axis=-1, keepdims=True)
    return zc * lax.rsqrt(var + LN_EPS) * g + b


def _split3(v):
    p0 = v.astype(BF16)
    r = v - p0.astype(F32)
    p1 = r.astype(BF16)
    p2 = (r - p1.astype(F32)).astype(BF16)
    return p0, p1, p2


def _mm_kernel(x_ref, w_ref, o_ref):
    o_ref[...] = _dot(x_ref[...].astype(BF16), w_ref[...]).astype(o_ref.dtype)


def _matmul(x, w, out_dtype):
    s, k = x.shape
    n = w.shape[1]
    return pl.pallas_call(
        _mm_kernel,
        grid=(s // ROW_BLOCK,),
        in_specs=[pl.BlockSpec((ROW_BLOCK, k), lambda i: (i, 0)), _resident((k, n))],
        out_specs=pl.BlockSpec((ROW_BLOCK, n), lambda i: (i, 0)),
        out_shape=jax.ShapeDtypeStruct((s, n), out_dtype),
        compiler_params=_params("parallel"),
        name="matmul",
    )(x, w)


def _proj_ln_kernel(x_ref, y_ref, w_ref, g_ref, b_ref, o_ref):
    m = _dot(y_ref[...], w_ref[...])
    o_ref[...] = _layer_norm(DN_ALPHA * x_ref[...] + m, g_ref[...], b_ref[...])


def _proj_ln(x, y, w, g, b):
    s, d = x.shape
    row = pl.BlockSpec((ROW_BLOCK, d), lambda i: (i, 0))
    return pl.pallas_call(
        _proj_ln_kernel,
        grid=(s // ROW_BLOCK,),
        in_specs=[row, row, _resident((d, d)), _resident((1, d)), _resident((1, d))],
        out_specs=row,
        out_shape=jax.ShapeDtypeStruct((s, d), F32),
        compiler_params=_params("parallel"),
        name="proj_ln",
    )(x, y, w, g, b)


def _ffn_ln_kernel(x_ref, w1_ref, w3_ref, w2_ref, g_ref, b_ref, o_ref):
    x = x_ref[...]
    xb = x.astype(BF16)
    acc = jnp.zeros(x.shape, F32)
    for c in range(D_FF // FF_CHUNK):
        cols = slice(c * FF_CHUNK, (c + 1) * FF_CHUNK)
        a = _dot(xb, w1_ref[:, cols])
        b = _dot(xb, w3_ref[:, cols])
        gate = (a * jax.nn.sigmoid(a) * b).astype(BF16)
        acc = acc + _dot(gate, w2_ref[cols, :])
    o_ref[...] = _layer_norm(DN_ALPHA * x + acc, g_ref[...], b_ref[...])


def _ffn_ln(x, w1, w3, w2, g, b):
    s, d = x.shape
    row = pl.BlockSpec((ROW_BLOCK, d), lambda i: (i, 0))
    return pl.pallas_call(
        _ffn_ln_kernel,
        grid=(s // ROW_BLOCK,),
        in_specs=[row, _resident((d, D_FF)), _resident((d, D_FF)), _resident((D_FF, d)),
                  _resident((1, d)), _resident((1, d))],
        out_specs=row,
        out_shape=jax.ShapeDtypeStruct((s, d), F32),
        compiler_params=_params("parallel"),
        name="ffn_ln",
    )(x, w1, w3, w2, g, b)


def _fox_qkv_kernel(x_ref, wqk_ref, wv_ref, wf_ref, bf_ref, tri_ref, e_ref, ones_ref,
                    q_ref, k_ref, v_ref, carry_ref):
    @pl.when(pl.program_id(0) == 0)
    def _():
        carry_ref[...] = jnp.zeros_like(carry_ref)

    xb = x_ref[...].astype(BF16)
    z = _dot(xb, wf_ref[...]) + bf_ref[...]
    log_f = jnp.minimum(z, 0.0) - jnp.log1p(jnp.exp(-jnp.abs(z)))
    tri = tri_ref[...]
    p0, p1, p2 = _split3(log_f)
    c = _dot(tri, p0) + _dot(tri, p1) + _dot(tri, p2) + carry_ref[...]
    carry_ref[...] = c[ROW_BLOCK - 1:ROW_BLOCK, :]
    c3 = jnp.concatenate(_split3(c), axis=1)
    qk = _dot(xb, wqk_ref[...]) + _dot(c3, e_ref[...]) + ones_ref[...]
    nq = FOX_HEADS * HEAD_SLOT
    q_ref[...] = qk[:, :nq].astype(BF16)
    k_ref[...] = qk[:, nq:].astype(BF16)
    v_ref[...] = _dot(xb, wv_ref[...]).astype(BF16)


def _fox_qkv(x, w_qkv, w_f, b_f):
    s, d = x.shape
    nq = FOX_HEADS * HEAD_SLOT
    dh = FOX_HEAD_DIM
    scale = dh ** -0.5
    wq = (w_qkv[:, :d] * scale).reshape(d, FOX_HEADS, dh)
    wk = w_qkv[:, d:2 * d].reshape(d, FOX_HEADS, dh)
    pad = jnp.zeros((d, FOX_HEADS, HEAD_SLOT - dh), F32)
    wqk = jnp.concatenate([jnp.concatenate([wq, pad], 2).reshape(d, nq),
                           jnp.concatenate([wk, pad], 2).reshape(d, nq)], 1).astype(BF16)
    wv = w_qkv[:, 2 * d:].astype(BF16)
    wf = jnp.pad(w_f, ((0, 0), (0, LANES - FOX_HEADS))).astype(BF16)
    bf = jnp.pad(b_f, (0, LANES - FOX_HEADS)).reshape(1, LANES)

    heads = jnp.arange(FOX_HEADS)
    e = jnp.zeros((3, LANES, 2 * nq), F32)
    ones = jnp.zeros((1, 2 * nq), F32)
    for i in range(3):
        e = e.at[i, heads, heads * HEAD_SLOT + dh + i].set(1.0)
        e = e.at[i, heads, nq + heads * HEAD_SLOT + dh + 3 + i].set(-1.0)
        ones = ones.at[0, heads * HEAD_SLOT + dh + 3 + i].set(1.0)
        ones = ones.at[0, nq + heads * HEAD_SLOT + dh + i].set(1.0)
    e = e.reshape(3 * LANES, 2 * nq).astype(BF16)
    r = jnp.arange(ROW_BLOCK)
    tri = (r[:, None] >= r[None, :]).astype(BF16)

    return pl.pallas_call(
        _fox_qkv_kernel,
        grid=(s // ROW_BLOCK,),
        in_specs=[pl.BlockSpec((ROW_BLOCK, d), lambda i: (i, 0)),
                  _resident((d, 2 * nq)), _resident((d, d)), _resident((d, LANES)),
                  _resident((1, LANES)), _resident((ROW_BLOCK, ROW_BLOCK)),
                  _resident((3 * LANES, 2 * nq)), _resident((1, 2 * nq))],
        out_specs=[pl.BlockSpec((ROW_BLOCK, nq), lambda i: (i, 0)),
                   pl.BlockSpec((ROW_BLOCK, nq), lambda i: (i, 0)),
                   pl.BlockSpec((ROW_BLOCK, d), lambda i: (i, 0))],
        out_shape=[jax.ShapeDtypeStruct((s, nq), BF16), jax.ShapeDtypeStruct((s, nq), BF16),
                   jax.ShapeDtypeStruct((s, d), BF16)],
        scratch_shapes=[pltpu.VMEM((1, LANES), F32)],
        compiler_params=_params("arbitrary"),
        name="fox_qkv",
    )(x, wqk, wv, wf, bf, tri, e, ones)


def _fox_attn_kernel(q_ref, k_ref, v_ref, o_ref, m_ref, l_ref, acc_ref):
    i = pl.program_id(1)
    t = ATT_BLOCK
    m_ref[...] = jnp.full(m_ref.shape, NEG_BIG, F32)
    l_ref[...] = jnp.zeros(l_ref.shape, F32)
    acc_ref[...] = jnp.zeros(acc_ref.shape, F32)

    def step(j, masked):
        rows = pl.ds(pl.multiple_of(j * t, t), t)
        v = v_ref[rows, :]
        for h in range(2):
            lanes = slice(h * HEAD_SLOT, (h + 1) * HEAD_SLOT)
            s = lax.dot_general(q_ref[:, lanes], k_ref[rows, lanes],
                                (((1,), (1,)), ((), ())), preferred_element_type=F32)
            if masked:
                r = lax.broadcasted_iota(jnp.int32, (t, t), 0)
                c = lax.broadcasted_iota(jnp.int32, (t, t), 1)
                s = jnp.where(c <= r, s, NEG_BIG)
            m_old = m_ref[h]
            m_new = jnp.maximum(m_old, jnp.max(s, axis=-1, keepdims=True))
            a = jnp.exp(m_old - m_new)
            p = jnp.exp(s - m_new)
            l_ref[h] = a * l_ref[h] + jnp.sum(p, axis=-1, keepdims=True)
            acc_ref[h] = a * acc_ref[h] + _dot(p.astype(BF16), v)
            m_ref[h] = m_new

    def body(j, carry):
        step(j, False)
        return carry

    lax.fori_loop(0, i, body, 0)
    step(i, True)

    o0 = acc_ref[0] / l_ref[0]
    o1 = acc_ref[1] / l_ref[1]
    lane = lax.broadcasted_iota(jnp.int32, o0.shape, 1)
    o_ref[...] = jnp.where(lane < FOX_HEAD_DIM, o0, o1).astype(o_ref.dtype)


def _fox_attn(q, k, v):
    s = q.shape[0]
    t = ATT_BLOCK
    pair = 2 * HEAD_SLOT
    return pl.pallas_call(
        _fox_attn_kernel,
        grid=(FOX_HEADS // 2, s // t),
        in_specs=[pl.BlockSpec((t, pair), lambda hp, i: (i, hp)),
                  pl.BlockSpec((s, pair), lambda hp, i: (0, hp)),
                  pl.BlockSpec((s, LANES), lambda hp, i: (0, hp))],
        out_specs=pl.BlockSpec((t, LANES), lambda hp, i: (i, hp)),
        out_shape=jax.ShapeDtypeStruct((s, D_MODEL), BF16),
        scratch_shapes=[pltpu.VMEM((2, t, 1), F32), pltpu.VMEM((2, t, 1), F32),
                        pltpu.VMEM((2, t, LANES), F32)],
        compiler_params=_params("parallel", "arbitrary"),
        name="fox_attn",
    )(q, k, v)


def _fox_layer(h, w_qkv, w_f, b_f, w_o, g, b):
    q, k, v = _fox_qkv(h, w_qkv, w_f, b_f)
    o = _fox_attn(q, k, v)
    return _proj_ln(h, o, w_o.astype(BF16), g, b)


def _s5_scan_kernel(u_ref, bm_ref, cm_ref, lpr_ref, lpi_ref, pwr_ref, pwi_ref, d_ref,
                    o_ref, hr_ref, hi_ref):
    @pl.when(pl.program_id(0) == 0)
    def _():
        hr_ref[...] = jnp.zeros_like(hr_ref)
        hi_ref[...] = jnp.zeros_like(hi_ref)

    n = S5_CHUNK
    st = S5_GB_ST
    row = lax.broadcasted_iota(jnp.int32, (n, st), 0)
    for gb in range(S5_GROUPS // S5_GB):
        ch = slice(gb * S5_GB_CH, (gb + 1) * S5_GB_CH)
        u = u_ref[:, ch]
        bu = _dot(u.astype(BF16), bm_ref[gb])
        hr, hi = bu[:, :st], bu[:, st:]
        for lvl in range(int(math.log2(n))):
            dist = 1 << lvl
            keep = row >= dist
            sr = jnp.where(keep, pltpu.roll(hr, dist, 0), 0.0)
            si = jnp.where(keep, pltpu.roll(hi, dist, 0), 0.0)
            pr = lpr_ref[gb, lvl:lvl + 1, :]
            pi = lpi_ref[gb, lvl:lvl + 1, :]
            hr, hi = hr + pr * sr - pi * si, hi + pr * si + pi * sr
        h0r, h0i = hr_ref[gb], hi_ref[gb]
        pwr, pwi = pwr_ref[gb], pwi_ref[gb]
        hr, hi = hr + pwr * h0r - pwi * h0i, hi + pwr * h0i + pwi * h0r
        hr_ref[gb] = hr[n - 1:n, :]
        hi_ref[gb] = hi[n - 1:n, :]
        hcat = jnp.concatenate([hr, hi], axis=1).astype(BF16)
        y = _dot(hcat, cm_ref[gb]) + d_ref[:, ch] * u
        o_ref[:, ch] = jax.nn.gelu(y).astype(o_ref.dtype)


def _s5_tables(a_re, a_im, log_dt, b_re, b_im, c_re, c_im):
    dt = jnp.exp(log_dt)[:, None]
    mag = jnp.exp(a_re * dt)
    lr, li = mag * jnp.cos(a_im * dt), mag * jnp.sin(a_im * dt)
    den = a_re * a_re + a_im * a_im
    zr = ((lr - 1.0) * a_re + li * a_im) / den
    zi = (li * a_re - (lr - 1.0) * a_im) / den
    bbr = zr[..., None] * b_re - zi[..., None] * b_im
    bbi = zr[..., None] * b_im + zi[..., None] * b_re

    ngb = S5_GROUPS // S5_GB
    eye = jnp.eye(S5_GB, dtype=F32)

    def blockdiag_in(bb):
        t = bb.reshape(ngb, S5_GB, S5_STATE, S5_GROUP).transpose(0, 1, 3, 2)
        t = t[:, :, :, None, :] * eye[None, :, None, :, None]
        return t.reshape(ngb, S5_GB_CH, S5_GB_ST)

    def blockdiag_out(cc):
        t = cc.reshape(ngb, S5_GB, S5_GROUP, S5_STATE).transpose(0, 1, 3, 2)
        t = t[:, :, :, None, :] * eye[None, :, None, :, None]
        return t.reshape(ngb, S5_GB_ST, S5_GB_CH)

    bm = jnp.concatenate([blockdiag_in(bbr), blockdiag_in(bbi)], axis=2).astype(BF16)
    cm = jnp.concatenate([blockdiag_out(c_re), blockdiag_out(-c_im)], axis=1).astype(BF16)

    n = S5_CHUNK

    def cmul(x, y):
        return (y[0] * x[0] - y[1] * x[1], y[0] * x[1] + y[1] * x[0])

    lam_r = jnp.broadcast_to(lr.reshape(1, ngb, S5_GB_ST), (n, ngb, S5_GB_ST))
    lam_i = jnp.broadcast_to(li.reshape(1, ngb, S5_GB_ST), (n, ngb, S5_GB_ST))
    pw_r, pw_i = lax.associative_scan(cmul, (lam_r, lam_i), axis=0)
    pwr = pw_r.transpose(1, 0, 2)
    pwi = pw_i.transpose(1, 0, 2)
    lvls = [(1 << k) - 1 for k in range(int(math.log2(n)))]
    lpr = jnp.stack([pw_r[k] for k in lvls] + [pw_r[0]], axis=1)
    lpi = jnp.stack([pw_i[k] for k in lvls] + [pw_i[0]], axis=1)
    return bm, cm, lpr, lpi, pwr, pwi


def _s5_scan(u, tables, d_skip):
    s, d = u.shape
    bm, cm, lpr, lpi, pwr, pwi = tables
    ngb = S5_GROUPS // S5_GB
    n = S5_CHUNK
    row = pl.BlockSpec((n, d), lambda i: (i, 0))
    return pl.pallas_call(
        _s5_scan_kernel,
        grid=(s // n,),
        in_specs=[row, _resident(bm.shape), _resident(cm.shape), _resident(lpr.shape),
                  _resident(lpi.shape), _resident(pwr.shape), _resident(pwi.shape),
                  _resident((1, d))],
        out_specs=row,
        out_shape=jax.ShapeDtypeStruct((s, d), BF16),
        scratch_shapes=[pltpu.VMEM((ngb, 1, S5_GB_ST), F32), pltpu.VMEM((ngb, 1, S5_GB_ST), F32)],
        compiler_params=_params("arbitrary"),
        name="s5_scan",
    )(u, bm, cm, lpr, lpi, pwr, pwi, d_skip.reshape(1, d))


def _glu_ln_kernel(x_ref, y_ref, w_ref, bias_ref, g_ref, b_ref, o_ref):
    d = x_ref.shape[1]
    z = _dot(y_ref[...], w_ref[...]) + bias_ref[...]
    m = z[:, :d] * jax.nn.sigmoid(z[:, d:])
    o_ref[...] = _layer_norm(DN_ALPHA * x_ref[...] + m, g_ref[...], b_ref[...])


def _glu_ln(x, y, w, bias, g, b):
    s, d = x.shape
    row = pl.BlockSpec((ROW_BLOCK, d), lambda i: (i, 0))
    return pl.pallas_call(
        _glu_ln_kernel,
        grid=(s // ROW_BLOCK,),
        in_specs=[row, row, _resident((d, 2 * d)), _resident((1, 2 * d)),
                  _resident((1, d)), _resident((1, d))],
        out_specs=row,
        out_shape=jax.ShapeDtypeStruct((s, d), F32),
        compiler_params=_params("parallel"),
        name="glu_ln",
    )(x, y, w, bias, g, b)


def _s5_layer(h, w_in, a_re, a_im, log_dt, b_re, b_im, c_re, c_im, d_skip, w_glu, b_glu, g, b):
    u = _matmul(h, w_in.astype(BF16), F32)
    y = _s5_scan(u, _s5_tables(a_re, a_im, log_dt, b_re, b_im, c_re, c_im), d_skip)
    return _glu_ln(h, y, w_glu.astype(BF16), b_glu.reshape(1, -1), g, b)


def _pool_ln_kernel(x_ref, win_ref, wgrp_ref, scale_ref, wout_ref, g_ref, b_ref, o_ref, ext_ref):
    i = pl.program_id(0)
    n = ROW_BLOCK

    @pl.when(i == 0)
    def _():
        ext_ref[0:POOL_HALO, :] = jnp.zeros((POOL_HALO, D_MODEL), F32)

    x = x_ref[...]
    u = _dot(x.astype(BF16), win_ref[...])
    ext_ref[POOL_HALO:, :] = u
    acc = ext_ref[...]
    pos1 = (i * n + 1 + lax.broadcasted_iota(jnp.int32, (n, 1), 0)).astype(F32)
    mixed = []
    for gi, w in enumerate(POOL_WINDOWS):
        acc = acc[:, (POOL_GROUP if gi else 0):]
        acc = acc + pltpu.roll(acc, w // 2, 0)
        cols = slice(gi * POOL_GROUP, (gi + 1) * POOL_GROUP)
        mean = acc[POOL_HALO:, :POOL_GROUP] / jnp.minimum(pos1, float(w))
        m = (mean - u[:, cols]).astype(BF16)
        mixed.append(_dot(m, wgrp_ref[gi]))
    m = (jnp.concatenate(mixed, axis=1) * scale_ref[...]).astype(BF16)
    y = _dot(m, wout_ref[...])
    o_ref[...] = _layer_norm(DN_ALPHA * x + y, g_ref[...], b_ref[...])
    ext_ref[0:POOL_HALO, :] = u[n - POOL_HALO:, :]


def _pool_layer(h, w_in, w_grp, scale, w_out, g, b):
    s, d = h.shape
    row = pl.BlockSpec((ROW_BLOCK, d), lambda i: (i, 0))
    return pl.pallas_call(
        _pool_ln_kernel,
        grid=(s // ROW_BLOCK,),
        in_specs=[row, _resident((d, d)), _resident(w_grp.shape), _resident((1, d)),
                  _resident((d, d)), _resident((1, d)), _resident((1, d))],
        out_specs=row,
        out_shape=jax.ShapeDtypeStruct((s, d), F32),
        scratch_shapes=[pltpu.VMEM((ROW_BLOCK + POOL_HALO, d), F32)],
        compiler_params=_params("arbitrary"),
        name="pool_ln",
    )(h, w_in.astype(BF16), w_grp.astype(BF16), scale.reshape(1, d), w_out.astype(BF16), g, b)


def kernel(x, ln_g, ln_b, fox_w_qkv, fox_w_f, fox_b_f, fox_w_o, s5_w_in, s5_a_re, s5_a_im, s5_log_dt, s5_b_re, s5_b_im, s5_c_re, s5_c_im, s5_d, s5_w_glu, s5_b_glu, pool_w_in, pool_w_grp, pool_scale, pool_w_out, ffn_w1, ffn_w3, ffn_w2):
    bsz, s, d = x.shape
    outs = []
    for bi in range(bsz):
        h = x[bi]
        for i in range(DEPTH):
            kind, j = i % N_MIXERS, i // N_MIXERS
            g0, b0 = ln_g[i, 0].reshape(1, d), ln_b[i, 0].reshape(1, d)
            g1, b1 = ln_g[i, 1].reshape(1, d), ln_b[i, 1].reshape(1, d)
            if kind == 0:
                h = _fox_layer(h, fox_w_qkv[j], fox_w_f[j], fox_b_f[j], fox_w_o[j], g0, b0)
            elif kind == 1:
                h = _s5_layer(h, s5_w_in[j], s5_a_re[j], s5_a_im[j], s5_log_dt[j],
                              s5_b_re[j], s5_b_im[j], s5_c_re[j], s5_c_im[j],
                              s5_d[j], s5_w_glu[j], s5_b_glu[j], g0, b0)
            else:
                h = _pool_layer(h, pool_w_in[j], pool_w_grp[j], pool_scale[j], pool_w_out[j], g0, b0)
            h = _ffn_ln(h, ffn_w1[i].astype(BF16), ffn_w3[i].astype(BF16), ffn_w2[i].astype(BF16),
                        g1, b1)
        outs.append(h)
    return jnp.stack(outs, axis=0)
```

```python
import functools
import math

import jax
import jax.numpy as jnp
from jax import lax
from jax.experimental import pallas as pl
from jax.experimental.pallas import tpu as pltpu

F32 = jnp.float32
BF16 = jnp.bfloat16

D_MODEL = 1024
SEQ = 16384
DEPTH = 4
N_MIXERS = 3
FOX_HEADS = 16
FOX_HEAD_DIM = D_MODEL // FOX_HEADS
S5_GROUP = 16
S5_GROUPS = D_MODEL // S5_GROUP
S5_STATE = 64
POOL_WINDOWS = (2, 4, 8, 16)
POOL_GROUP = D_MODEL // len(POOL_WINDOWS)
D_FF = 2816
DN_ALPHA = (2.0 * DEPTH) ** 0.25
LN_EPS = 1e-5

LANES = 128
VMEM_LIMIT = 56 * 1024 * 1024

ROW_BLOCK = 512
FF_CHUNK = 256
ATT_BLOCK = 512
HEAD_SLOT = 2 * FOX_HEAD_DIM
V_SLOT = 80
LOG2E = math.log2(math.e)
S5_CHUNK = 128
S5_GB = 8
S5_GB_CH = S5_GB * S5_GROUP
S5_GB_ST = S5_GB * S5_STATE
POOL_HALO = 16
NEG_BIG = -1e30


def _resident(shape):
    nd = len(shape)
    return pl.BlockSpec(shape, lambda *_: (0,) * nd, pipeline_mode=pl.Buffered(1))


def _params(*sem):
    return pltpu.CompilerParams(dimension_semantics=sem, vmem_limit_bytes=VMEM_LIMIT)


def _dot(a, b):
    return jnp.dot(a, b, preferred_element_type=F32)


def _layer_norm(z, g, b):
    mu = jnp.mean(z, axis=-1, keepdims=True)
    zc = z - mu
    var = jnp.mean(zc * zc, axis=-1, keepdims=True)
    return zc * lax.rsqrt(var + LN_EPS) * g + b


def _split3(v):
    p0 = v.astype(BF16)
    r = v - p0.astype(F32)
    p1 = r.astype(BF16)
    p2 = (r - p1.astype(F32)).astype(BF16)
    return p0, p1, p2


def _mm_kernel(x_ref, w_ref, o_ref):
    o_ref[...] = _dot(x_ref[...].astype(BF16), w_ref[...]).astype(o_ref.dtype)


def _matmul(x, w, out_dtype):
    s, k = x.shape
    n = w.shape[1]
    return pl.pallas_call(
        _mm_kernel,
        grid=(s // ROW_BLOCK,),
        in_specs=[pl.BlockSpec((ROW_BLOCK, k), lambda i: (i, 0)), _resident((k, n))],
        out_specs=pl.BlockSpec((ROW_BLOCK, n), lambda i: (i, 0)),
        out_shape=jax.ShapeDtypeStruct((s, n), out_dtype),
        compiler_params=_params("parallel"),
        name="matmul",
    )(x, w)


def _proj_ln_kernel(x_ref, y_ref, w_ref, g_ref, b_ref, o_ref):
    m = _dot(y_ref[...], w_ref[...])
    o_ref[...] = _layer_norm(DN_ALPHA * x_ref[...] + m, g_ref[...], b_ref[...])


def _proj_ln(x, y, w, g, b):
    s, d = x.shape
    row = pl.BlockSpec((ROW_BLOCK, d), lambda i: (i, 0))
    return pl.pallas_call(
        _proj_ln_kernel,
        grid=(s // ROW_BLOCK,),
        in_specs=[row, row, _resident((d, d)), _resident((1, d)), _resident((1, d))],
        out_specs=row,
        out_shape=jax.ShapeDtypeStruct((s, d), F32),
        compiler_params=_params("parallel"),
        name="proj_ln",
    )(x, y, w, g, b)


def _ffn_ln_kernel(x_ref, w1_ref, w3_ref, w2_ref, g_ref, b_ref, o_ref):
    x = x_ref[...]
    xb = x.astype(BF16)
    acc = jnp.zeros(x.shape, F32)
    for c in range(D_FF // FF_CHUNK):
        cols = slice(c * FF_CHUNK, (c + 1) * FF_CHUNK)
        a = _dot(xb, w1_ref[:, cols])
        b = _dot(xb, w3_ref[:, cols])
        gate = (a * jax.nn.sigmoid(a) * b).astype(BF16)
        acc = acc + _dot(gate, w2_ref[cols, :])
    o_ref[...] = _layer_norm(DN_ALPHA * x + acc, g_ref[...], b_ref[...])


def _ffn_ln(x, w1, w3, w2, g, b):
    s, d = x.shape
    row = pl.BlockSpec((ROW_BLOCK, d), lambda i: (i, 0))
    return pl.pallas_call(
        _ffn_ln_kernel,
        grid=(s // ROW_BLOCK,),
        in_specs=[row, _resident((d, D_FF)), _resident((d, D_FF)), _resident((D_FF, d)),
                  _resident((1, d)), _resident((1, d))],
        out_specs=row,
        out_shape=jax.ShapeDtypeStruct((s, d), F32),
        compiler_params=_params("parallel"),
        name="ffn_ln",
    )(x, w1, w3, w2, g, b)


def _fox_qkv_kernel(x_ref, wqk_ref, wvt_ref, vone_ref, wf_ref, bf_ref, tri_ref, e_ref, ones_ref,
                    q_ref, k_ref, vt_ref, carry_ref):
    @pl.when(pl.program_id(0) == 0)
    def _():
        carry_ref[...] = jnp.zeros_like(carry_ref)

    xb = x_ref[...].astype(BF16)
    z = _dot(xb, wf_ref[...]) + bf_ref[...]
    log_f = jnp.minimum(z, 0.0) - jnp.log1p(jnp.exp(-jnp.abs(z)))
    tri = tri_ref[...]
    p0, p1, p2 = _split3(log_f)
    c = _dot(tri, p0) + _dot(tri, p1) + _dot(tri, p2) + carry_ref[...]
    carry_ref[...] = c[ROW_BLOCK - 1:ROW_BLOCK, :]
    c3 = jnp.concatenate(_split3(c * LOG2E), axis=1)
    qk = _dot(xb, wqk_ref[...]) + _dot(c3, e_ref[...]) + ones_ref[...]
    nq = FOX_HEADS * HEAD_SLOT
    q_ref[...] = qk[:, :nq].astype(BF16)
    k_ref[...] = qk[:, nq:].astype(BF16)
    vt = lax.dot_general(wvt_ref[...], xb, (((1,), (1,)), ((), ())), preferred_element_type=F32)
    vt_ref[0] = (vt + vone_ref[...]).astype(BF16)


def _fox_qkv(x, w_qkv, w_f, b_f):
    s, d = x.shape
    nq = FOX_HEADS * HEAD_SLOT
    nv = FOX_HEADS * V_SLOT
    dh = FOX_HEAD_DIM
    scale = dh ** -0.5 * LOG2E
    wq = (w_qkv[:, :d] * scale).reshape(d, FOX_HEADS, dh)
    wk = w_qkv[:, d:2 * d].reshape(d, FOX_HEADS, dh)
    pad = jnp.zeros((d, FOX_HEADS, HEAD_SLOT - dh), F32)
    wqk = jnp.concatenate([jnp.concatenate([wq, pad], 2).reshape(d, nq),
                           jnp.concatenate([wk, pad], 2).reshape(d, nq)], 1).astype(BF16)
    wv = w_qkv[:, 2 * d:].T.reshape(FOX_HEADS, dh, d)
    wvt = jnp.pad(wv, ((0, 0), (0, V_SLOT - dh), (0, 0))).reshape(nv, d).astype(BF16)
    vone = jnp.zeros((FOX_HEADS, V_SLOT), F32).at[:, dh].set(1.0).reshape(nv, 1)
    wf =jnp.pad(w_f, ((0, 0), (0, LANES - FOX_HEADS))).astype(BF16)
    bf = jnp.pad(b_f, (0, LANES - FOX_HEADS)).reshape(1, LANES)

    heads = jnp.arange(FOX_HEADS)
    e = jnp.zeros((3, LANES, 2 * nq), F32)
    ones = jnp.zeros((1, 2 * nq), F32)
    for i in range(3):
        e = e.at[i, heads, heads * HEAD_SLOT + dh + i].set(1.0)
        e = e.at[i, heads, nq + heads * HEAD_SLOT + dh + 3 + i].set(-1.0)
        ones = ones.at[0, heads * HEAD_SLOT + dh + 3 + i].set(1.0)
        ones = ones.at[0, nq + heads * HEAD_SLOT + dh + i].set(1.0)
    e = e.reshape(3 * LANES, 2 * nq).astype(BF16)
    r = jnp.arange(ROW_BLOCK)
    tri = (r[:, None] >= r[None, :]).astype(BF16)

    nb = s // ROW_BLOCK
    return pl.pallas_call(
        _fox_qkv_kernel,
        grid=(nb,),
        in_specs=[pl.BlockSpec((ROW_BLOCK, d), lambda i: (i, 0)),
                  _resident((d, 2 * nq)), _resident((nv, d)), _resident((nv, 1)),
                  _resident((d, LANES)), _resident((1, LANES)),
                  _resident((ROW_BLOCK, ROW_BLOCK)),
                  _resident((3 * LANES, 2 * nq)), _resident((1, 2 * nq))],
        out_specs=[pl.BlockSpec((ROW_BLOCK, nq), lambda i: (i, 0)),
                   pl.BlockSpec((ROW_BLOCK, nq), lambda i: (i, 0)),
                   pl.BlockSpec((1, nv, ROW_BLOCK), lambda i: (i, 0, 0))],
        out_shape=[jax.ShapeDtypeStruct((s, nq), BF16), jax.ShapeDtypeStruct((s, nq), BF16),
                   jax.ShapeDtypeStruct((nb, nv, ROW_BLOCK), BF16)],
        scratch_shapes=[pltpu.VMEM((1, LANES), F32)],
        compiler_params=_params("arbitrary"),
        name="fox_qkv",
    )(x, wqk, wvt, vone, wf, bf, tri, e, ones)


def _fox_attn_kernel(q_ref, k_ref, vt_ref, o_ref, m_ref, acc_ref, s_ref, bm_ref):
    i = pl.program_id(1)
    t = ATT_BLOCK
    m_ref[...] = jnp.full(m_ref.shape, NEG_BIG, F32)
    acc_ref[...] = jnp.zeros(acc_ref.shape, F32)

    def scores(h, j):
        rows = pl.ds(pl.multiple_of(j * t, t), t)
        lanes = slice(h * HEAD_SLOT, (h + 1) * HEAD_SLOT)
        st = lax.dot_general(k_ref[rows, lanes], q_ref[:, lanes],
                             (((1,), (1,)), ((), ())), preferred_element_type=F32)
        s_ref[h] = st
        bm_ref[h] = jnp.max(st, axis=0, keepdims=True)

    def accumulate(h, j, masked):
        st = s_ref[h]
        if masked:
            key = lax.broadcasted_iota(jnp.int32, (t, t), 0)
            qry = lax.broadcasted_iota(jnp.int32, (t, t), 1)
            st = jnp.where(key <= qry, st, NEG_BIG)
            bm = jnp.max(st, axis=0, keepdims=True)
        else:
            bm = bm_ref[h]
        m_old = m_ref[h]
        m_new = jnp.maximum(m_old, bm)
        p = jnp.exp2(st - m_new).astype(BF16)
        pv = _dot(vt_ref[j, h * V_SLOT:(h + 1) * V_SLOT, :], p)
        acc_ref[h] = jnp.exp2(m_old - m_new) * acc_ref[h] + pv
        m_ref[h] = m_new

    scores(0, 0)

    def body(j, carry):
        scores(1, j)
        accumulate(0, j, False)
        scores(0, j + 1)
        accumulate(1, j, False)
        return carry

    lax.fori_loop(0, i, body, 0)
    scores(1, i)
    accumulate(0, i, True)
    accumulate(1, i, True)

    outs = []
    for h in range(2):
        acc = acc_ref[h]
        outs.append(acc[:FOX_HEAD_DIM] / acc[FOX_HEAD_DIM:FOX_HEAD_DIM + 1])
    o_ref[...] = jnp.concatenate(outs, axis=0).T.astype(o_ref.dtype)


def _fox_attn(q, k, vt):
    s = q.shape[0]
    t = ATT_BLOCK
    pair = 2 * HEAD_SLOT
    return pl.pallas_call(
        _fox_attn_kernel,
        grid=(FOX_HEADS // 2, s // t),
        in_specs=[pl.BlockSpec((t, pair), lambda hp, i: (i, hp)),
                  pl.BlockSpec((s, pair), lambda hp, i: (0, hp)),
                  pl.BlockSpec((s // t, 2 * V_SLOT, t), lambda hp, i: (0, hp, 0))],
        out_specs=pl.BlockSpec((t, LANES), lambda hp, i: (i, hp)),
        out_shape=jax.ShapeDtypeStruct((s, D_MODEL), BF16),
        scratch_shapes=[pltpu.VMEM((2, 1, t), F32), pltpu.VMEM((2, V_SLOT, t), F32),
                        pltpu.VMEM((2, t, t), F32), pltpu.VMEM((2, 1, t), F32)],
        compiler_params=_params("parallel", "arbitrary"),
        name="fox_attn",
    )(q, k, vt)


def _fox_layer(h, w_qkv, w_f, b_f, w_o, g, b):
    q, k, vt = _fox_qkv(h, w_qkv, w_f, b_f)
    o = _fox_attn(q, k, vt)
    return _proj_ln(h, o, w_o.astype(BF16), g, b)


def _s5_scan_kernel(u_ref, bm_ref, cm_ref, lpr_ref, lpi_ref, pwr_ref, pwi_ref, d_ref,
                    o_ref, hr_ref, hi_ref):
    @pl.when(pl.program_id(0) == 0)
    def _():
        hr_ref[...] = jnp.zeros_like(hr_ref)
        hi_ref[...] = jnp.zeros_like(hi_ref)

    n = S5_CHUNK
    st = S5_GB_ST
    row = lax.broadcasted_iota(jnp.int32, (n, st), 0)
    for gb in range(S5_GROUPS // S5_GB):
        ch = slice(gb * S5_GB_CH, (gb + 1) * S5_GB_CH)
        u = u_ref[:, ch]
        bu = _dot(u.astype(BF16), bm_ref[gb])
        hr, hi = bu[:, :st], bu[:, st:]
        for lvl in range(int(math.log2(n))):
            dist = 1 << lvl
            keep = row >= dist
            sr = jnp.where(keep, pltpu.roll(hr, dist, 0), 0.0)
            si = jnp.where(keep, pltpu.roll(hi, dist, 0), 0.0)
            pr = lpr_ref[gb, lvl:lvl + 1, :]
            pi = lpi_ref[gb, lvl:lvl + 1, :]
            hr, hi = hr + pr * sr - pi * si, hi + pr * si + pi * sr
        h0r, h0i = hr_ref[gb], hi_ref[gb]
        pwr, pwi = pwr_ref[gb], pwi_ref[gb]
        hr, hi = hr + pwr * h0r - pwi * h0i, hi + pwr * h0i + pwi * h0r
        hr_ref[gb] = hr[n - 1:n, :]
        hi_ref[gb] = hi[n - 1:n, :]
        hcat = jnp.concatenate([hr, hi], axis=1).astype(BF16)
        y = _dot(hcat, cm_ref[gb]) + d_ref[:, ch] * u
        o_ref[:, ch] = jax.nn.gelu(y).astype(o_ref.dtype)


def _s5_tables(a_re, a_im, log_dt, b_re, b_im, c_re, c_im):
    dt = jnp.exp(log_dt)[:, None]
    mag = jnp.exp(a_re * dt)
    lr, li = mag * jnp.cos(a_im * dt), mag * jnp.sin(a_im * dt)
    den = a_re * a_re + a_im * a_im
    zr = ((lr - 1.0) * a_re + li * a_im) / den
    zi = (li * a_re - (lr - 1.0) * a_im) / den
    bbr = zr[..., None] * b_re - zi[..., None] * b_im
    bbi = zr[..., None] * b_im + zi[..., None] * b_re

    ngb = S5_GROUPS // S5_GB
    eye = jnp.eye(S5_GB, dtype=F32)

    def blockdiag_in(bb):
        t = bb.reshape(ngb, S5_GB, S5_STATE, S5_GROUP).transpose(0, 1, 3, 2)
        t = t[:, :, :, None, :] * eye[None, :, None, :, None]
        return t.reshape(ngb, S5_GB_CH, S5_GB_ST)

    def blockdiag_out(cc):
        t = cc.reshape(ngb, S5_GB, S5_GROUP, S5_STATE).transpose(0, 1, 3, 2)
        t = t[:, :, :, None, :] * eye[None, :, None, :, None]
        return t.reshape(ngb, S5_GB_ST, S5_GB_CH)

    bm = jnp.concatenate([blockdiag_in(bbr), blockdiag_in(bbi)], axis=2).astype(BF16)
    cm = jnp.concatenate([blockdiag_out(c_re), blockdiag_out(-c_im)], axis=1).astype(BF16)

    n = S5_CHUNK

    def cmul(x, y):
        return (y[0] * x[0] - y[1] * x[1], y[0] * x[1] + y[1] * x[0])

    lam_r = jnp.broadcast_to(lr.reshape(1, ngb, S5_GB_ST), (n, ngb, S5_GB_ST))
    lam_i = jnp.broadcast_to(li.reshape(1, ngb, S5_GB_ST), (n, ngb, S5_GB_ST))
    pw_r, pw_i = lax.associative_scan(cmul, (lam_r, lam_i), axis=0)
    pwr = pw_r.transpose(1, 0, 2)
    pwi = pw_i.transpose(1, 0, 2)
    lvls = [(1 << k) - 1 for k in range(int(math.log2(n)))]
    lpr = jnp.stack([pw_r[k] for k in lvls] + [pw_r[0]], axis=1)
    lpi = jnp.stack([pw_i[k] for k in lvls] + [pw_i[0]], axis=1)
    return bm, cm, lpr, lpi, pwr, pwi


def _s5_scan(u, tables, d_skip):
    s, d = u.shape
    bm, cm, lpr, lpi, pwr, pwi = tables
    ngb = S5_GROUPS // S5_GB
    n = S5_CHUNK
    row = pl.BlockSpec((n, d), lambda i: (i, 0))
    return pl.pallas_call(
        _s5_scan_kernel,
        grid=(s // n,),
        in_specs=[row, _resident(bm.shape), _resident(cm.shape), _resident(lpr.shape),
                  _resident(lpi.shape), _resident(pwr.shape), _resident(pwi.shape),
                  _resident((1, d))],
        out_specs=row,
        out_shape=jax.ShapeDtypeStruct((s, d), BF16),
        scratch_shapes=[pltpu.VMEM((ngb, 1, S5_GB_ST), F32), pltpu.VMEM((ngb, 1, S5_GB_ST), F32)],
        compiler_params=_params("arbitrary"),
        name="s5_scan",
    )(u, bm, cm, lpr, lpi, pwr, pwi, d_skip.reshape(1, d))


def _glu_ln_kernel(x_ref, y_ref, w_ref, bias_ref, g_ref, b_ref, o_ref):
    d = x_ref.shape[1]
    z = _dot(y_ref[...], w_ref[...]) + bias_ref[...]
    m = z[:, :d] * jax.nn.sigmoid(z[:, d:])
    o_ref[...] = _layer_norm(DN_ALPHA * x_ref[...] + m, g_ref[...], b_ref[...])


def _glu_ln(x, y, w, bias, g, b):
    s, d = x.shape
    row = pl.BlockSpec((ROW_BLOCK, d), lambda i: (i, 0))
    return pl.pallas_call(
        _glu_ln_kernel,
        grid=(s // ROW_BLOCK,),
        in_specs=[row, row, _resident((d, 2 * d)), _resident((1, 2 * d)),
                  _resident((1, d)), _resident((1, d))],
        out_specs=row,
        out_shape=jax.ShapeDtypeStruct((s, d), F32),
        compiler_params=_params("parallel"),
        name="glu_ln",
    )(x, y, w, bias, g, b)


def _s5_layer(h, w_in, a_re, a_im, log_dt, b_re, b_im, c_re, c_im, d_skip, w_glu, b_glu, g, b):
    u = _matmul(h, w_in.astype(BF16), F32)
    y = _s5_scan(u, _s5_tables(a_re, a_im, log_dt, b_re, b_im, c_re, c_im), d_skip)
    return _glu_ln(h, y, w_glu.astype(BF16), b_glu.reshape(1, -1), g, b)


def _pool_ln_kernel(x_ref, win_ref, wgrp_ref, scale_ref, wout_ref, g_ref, b_ref, o_ref, ext_ref):
    i = pl.program_id(0)
    n = ROW_BLOCK

    @pl.when(i == 0)
    def _():
        ext_ref[0:POOL_HALO, :] = jnp.zeros((POOL_HALO, D_MODEL), F32)

    x = x_ref[...]
    u = _dot(x.astype(BF16), win_ref[...])
    ext_ref[POOL_HALO:, :] = u
    acc = ext_ref[...]
    pos1 = (i * n + 1 + lax.broadcasted_iota(jnp.int32, (n, 1), 0)).astype(F32)
    mixed = []
    for gi, w in enumerate(POOL_WINDOWS):
        acc = acc[:, (POOL_GROUP if gi else 0):]
        acc = acc + pltpu.roll(acc, w // 2, 0)
        cols = slice(gi * POOL_GROUP, (gi + 1) * POOL_GROUP)
        mean = acc[POOL_HALO:, :POOL_GROUP] / jnp.minimum(pos1, float(w))
        m = (mean - u[:, cols]).astype(BF16)
        mixed.append(_dot(m, wgrp_ref[gi]))
    m = (jnp.concatenate(mixed, axis=1) * scale_ref[...]).astype(BF16)
    y = _dot(m, wout_ref[...])
    o_ref[...] = _layer_norm(DN_ALPHA * x + y, g_ref[...], b_ref[...])
    ext_ref[0:POOL_HALO, :] = u[n - POOL_HALO:, :]


def _pool_layer(h, w_in, w_grp, scale, w_out, g, b):
    s, d = h.shape
    row = pl.BlockSpec((ROW_BLOCK, d), lambda i: (i, 0))
    return pl.pallas_call(
        _pool_ln_kernel,
        grid=(s // ROW_BLOCK,),
        in_specs=[row, _resident((d, d)), _resident(w_grp.shape), _resident((1, d)),
                  _resident((d, d)), _resident((1, d)), _resident((1, d))],
        out_specs=row,
        out_shape=jax.ShapeDtypeStruct((s, d), F32),
        scratch_shapes=[pltpu.VMEM((ROW_BLOCK + POOL_HALO, d), F32)],
        compiler_params=_params("arbitrary"),
        name="pool_ln",
    )(h, w_in.astype(BF16), w_grp.astype(BF16), scale.reshape(1, d), w_out.astype(BF16), g, b)


def kernel(x, ln_g, ln_b, fox_w_qkv, fox_w_f, fox_b_f, fox_w_o, s5_w_in, s5_a_re, s5_a_im, s5_log_dt, s5_b_re, s5_b_im, s5_c_re, s5_c_im, s5_d, s5_w_glu, s5_b_glu, pool_w_in, pool_w_grp, pool_scale, pool_w_out, ffn_w1, ffn_w3, ffn_w2):
    bsz, s, d = x.shape
    outs = []
    for bi in range(bsz):
        h = x[bi]
        for i in range(DEPTH):
            kind, j = i % N_MIXERS, i // N_MIXERS
            g0, b0 = ln_g[i, 0].reshape(1, d), ln_b[i, 0].reshape(1, d)
            g1, b1 = ln_g[i, 1].reshape(1, d), ln_b[i, 1].reshape(1, d)
            if kind == 0:
                h = _fox_layer(h, fox_w_qkv[j], fox_w_f[j], fox_b_f[j], fox_w_o[j], g0, b0)
            elif kind == 1:
                h = _s5_layer(h, s5_w_in[j], s5_a_re[j], s5_a_im[j], s5_log_dt[j],
                              s5_b_re[j], s5_b_im[j], s5_c_re[j], s5_c_im[j],
                              s5_d[j], s5_w_glu[j], s5_b_glu[j], g0, b0)
            else:
                h = _pool_layer(h, pool_w_in[j], pool_w_grp[j], pool_scale[j], pool_w_out[j], g0, b0)
            h = _ffn_ln(h, ffn_w1[i].astype(BF16), ffn_w3[i].astype(BF16), ffn_w2[i].astype(BF16),
                        g1, b1)
        outs.append(h)
    return jnp.stack(outs, axis=0)
```

```python
import functools
import math

import jax
import jax.numpy as jnp
from jax import lax
from jax.experimental import pallas as pl
from jax.experimental.pallas import tpu as pltpu

F32 = jnp.float32
BF16 = jnp.bfloat16

D_MODEL = 1024
SEQ = 16384
DEPTH = 4
N_MIXERS = 3
FOX_HEADS = 16
FOX_HEAD_DIM = D_MODEL // FOX_HEADS
S5_GROUP = 16
S5_GROUPS = D_MODEL // S5_GROUP
S5_STATE = 64
POOL_WINDOWS = (2, 4, 8, 16)
POOL_GROUP = D_MODEL // len(POOL_WINDOWS)
D_FF = 2816
DN_ALPHA = (2.0 * DEPTH) ** 0.25
LN_EPS = 1e-5

LANES = 128
VMEM_LIMIT = 56 * 1024 * 1024

ROW_BLOCK = 512
FF_CHUNK = 256
ATT_BLOCK = 512
KEY_TILE = 1024
HEAD_SLOT = 2 * FOX_HEAD_DIM
V_SLOT = 80
LOG2E = math.log2(math.e)
S5_ROWS = 256
S5_SKEW = 7
S5_HALO = 8
S5_PITCH = S5_ROWS + S5_HALO
S5_SLABS = 64
S5_GB = 8
S5_GB_CH = S5_GB * S5_GROUP
S5_GB_ST = S5_GB * S5_STATE
POOL_HALO = 16
NEG_BIG = -1e30


def _resident(shape):
    nd = len(shape)
    return pl.BlockSpec(shape, lambda *_: (0,) * nd, pipeline_mode=pl.Buffered(1))


def _params(*sem):
    return pltpu.CompilerParams(dimension_semantics=sem, vmem_limit_bytes=VMEM_LIMIT)


def _dot(a, b):
    return jnp.dot(a, b, preferred_element_type=F32)


def _layer_norm(z, g, b):
    mu = jnp.mean(z, axis=-1, keepdims=True)
    zc = z - mu
    var = jnp.mean(zc * zc, axis=-1, keepdims=True)
    return zc * lax.rsqrt(var + LN_EPS) * g + b


def _split3(v):
    p0 = v.astype(BF16)
    r = v - p0.astype(F32)
    p1 = r.astype(BF16)
    p2 = (r - p1.astype(F32)).astype(BF16)
    return p0, p1, p2


def _proj_ln_kernel(x_ref, y_ref, w_ref, g_ref, b_ref, o_ref):
    m = _dot(y_ref[...], w_ref[...])
    o_ref[...] = _layer_norm(DN_ALPHA * x_ref[...] + m, g_ref[...], b_ref[...])


def _proj_ln(x, y, w, g, b):
    s, d = x.shape
    row = pl.BlockSpec((ROW_BLOCK, d), lambda i: (i, 0))
    return pl.pallas_call(
        _proj_ln_kernel,
        grid=(s // ROW_BLOCK,),
        in_specs=[row, row, _resident((d, d)), _resident((1, d)), _resident((1, d))],
        out_specs=row,
        out_shape=jax.ShapeDtypeStruct((s, d), F32),
        compiler_params=_params("parallel"),
        name="proj_ln",
    )(x, y, w, g, b)


def _ffn_ln_kernel(x_ref, w1_ref, w3_ref, w2_ref, g_ref, b_ref, o_ref):
    x = x_ref[...]
    xb = x.astype(BF16)
    acc = jnp.zeros(x.shape, F32)
    for c in range(D_FF // FF_CHUNK):
        cols = slice(c * FF_CHUNK, (c + 1) * FF_CHUNK)
        a = _dot(xb, w1_ref[:, cols])
        b = _dot(xb, w3_ref[:, cols])
        gate = (a * jax.nn.sigmoid(a) * b).astype(BF16)
        acc = acc + _dot(gate, w2_ref[cols, :])
    o_ref[...] = _layer_norm(DN_ALPHA * x + acc, g_ref[...], b_ref[...])


def _ffn_ln(x, w1, w3, w2, g, b):
    s, d = x.shape
    row = pl.BlockSpec((ROW_BLOCK, d), lambda i: (i, 0))
    return pl.pallas_call(
        _ffn_ln_kernel,
        grid=(s // ROW_BLOCK,),
        in_specs=[row, _resident((d, D_FF)), _resident((d, D_FF)), _resident((D_FF, d)),
                  _resident((1, d)), _resident((1, d))],
        out_specs=row,
        out_shape=jax.ShapeDtypeStruct((s, d), F32),
        compiler_params=_params("parallel"),
        name="ffn_ln",
    )(x, w1, w3, w2, g, b)


def _fox_qkv_kernel(x_ref, wqt_ref, eqt_ref, oneq_ref, wk_ref, ek_ref, onek_ref, wvt_ref, vone_ref,
                    wf_ref, bf_ref, tri_ref, qt_ref, k_ref, vt_ref, carry_ref):
    @pl.when(pl.program_id(0) == 0)
    def _():
        carry_ref[...] = jnp.zeros_like(carry_ref)

    nt = (((1,), (1,)), ((), ()))
    xb = x_ref[...].astype(BF16)
    z = _dot(xb, wf_ref[...]) + bf_ref[...]
    log_f = jnp.minimum(z, 0.0) - jnp.log1p(jnp.exp(-jnp.abs(z)))
    tri = tri_ref[...]
    p0, p1, p2 = _split3(log_f)
    c = _dot(tri, p0) + _dot(tri, p1) + _dot(tri, p2) + carry_ref[...]
    carry_ref[...] = c[ROW_BLOCK - 1:ROW_BLOCK, :]
    c3 = jnp.concatenate(_split3(c * LOG2E), axis=1)
    k_ref[...] = (_dot(xb, wk_ref[...]) + _dot(c3, ek_ref[...]) + onek_ref[...]).astype(BF16)
    qt = (lax.dot_general(wqt_ref[...], xb, nt, preferred_element_type=F32)
          + lax.dot_general(eqt_ref[...], c3, nt, preferred_element_type=F32))
    qt_ref[0] = (qt + oneq_ref[...]).astype(BF16)
    vt = lax.dot_general(wvt_ref[...], xb, nt, preferred_element_type=F32)
    vt_ref[0] = (vt + vone_ref[...]).astype(BF16)


def _fox_qkv(x, w_qkv, w_f, b_f):
    s, d = x.shape
    nq = FOX_HEADS * HEAD_SLOT
    nv = FOX_HEADS * V_SLOT
    dh = FOX_HEAD_DIM
    scale = dh ** -0.5 * LOG2E

    def slots(w):
        w = w.reshape(d, FOX_HEADS, dh)
        return jnp.pad(w, ((0, 0), (0, 0), (0, HEAD_SLOT - dh))).reshape(d, nq)

    wqt = slots(w_qkv[:, :d] * scale).T.astype(BF16)
    wk = slots(w_qkv[:, d:2 * d]).astype(BF16)
    wv = w_qkv[:, 2 * d:].T.reshape(FOX_HEADS, dh, d)
    wvt = jnp.pad(wv, ((0, 0), (0, V_SLOT - dh), (0, 0))).reshape(nv, d).astype(BF16)
    vone = jnp.zeros((FOX_HEADS, V_SLOT), F32).at[:, dh].set(1.0).reshape(nv, 1)
    wf =jnp.pad(w_f, ((0, 0), (0, LANES - FOX_HEADS))).astype(BF16)
    bf = jnp.pad(b_f, (0, LANES - FOX_HEADS)).reshape(1, LANES)

    heads = jnp.arange(FOX_HEADS)
    eq = jnp.zeros((3, LANES, nq), F32)
    ek = jnp.zeros((3, LANES, nq), F32)
    oneq = jnp.zeros((nq,), F32)
    onek = jnp.zeros((nq,), F32)
    for i in range(3):
        eq = eq.at[i, heads, heads * HEAD_SLOT + dh + i].set(1.0)
        ek = ek.at[i, heads, heads * HEAD_SLOT + dh + 3 + i].set(-1.0)
        oneq = oneq.at[heads * HEAD_SLOT + dh + 3 + i].set(1.0)
        onek = onek.at[heads * HEAD_SLOT + dh + i].set(1.0)
    eqt = eq.reshape(3 * LANES, nq).T.astype(BF16)
    ek = ek.reshape(3 * LANES, nq).astype(BF16)
    r = jnp.arange(ROW_BLOCK)
    tri = (r[:, None] >= r[None, :]).astype(BF16)

    nb = s // ROW_BLOCK
    return pl.pallas_call(
        _fox_qkv_kernel,
        grid=(nb,),
        in_specs=[pl.BlockSpec((ROW_BLOCK, d), lambda i: (i, 0)),
                  _resident((nq, d)), _resident((nq, 3 * LANES)), _resident((nq, 1)),
                  _resident((d, nq)), _resident((3 * LANES, nq)), _resident((1, nq)),
                  _resident((nv, d)), _resident((nv, 1)),
                  _resident((d, LANES)), _resident((1, LANES)),
                  _resident((ROW_BLOCK, ROW_BLOCK))],
        out_specs=[pl.BlockSpec((1, nq, ROW_BLOCK), lambda i: (i, 0, 0)),
                   pl.BlockSpec((ROW_BLOCK, nq), lambda i: (i, 0)),
                   pl.BlockSpec((1, nv, ROW_BLOCK), lambda i: (i, 0, 0))],
        out_shape=[jax.ShapeDtypeStruct((nb, nq, ROW_BLOCK), BF16),
                   jax.ShapeDtypeStruct((s, nq), BF16),
                   jax.ShapeDtypeStruct((nb, nv, ROW_BLOCK), BF16)],
        scratch_shapes=[pltpu.VMEM((1, LANES), F32)],
        compiler_params=_params("arbitrary"),
        name="fox_qkv",
    )(x, wqt, eqt, oneq.reshape(nq, 1), wk, ek, onek.reshape(1, nq), wvt, vone, wf, bf, tri)


def _fox_attn_kernel(qt_ref, k_ref, vt_ref, o_ref, m_ref, acc_ref, s_ref, bm_ref):
    i = pl.program_id(1)
    t = ATT_BLOCK
    per_tile = KEY_TILE // t
    m_ref[...] = jnp.full(m_ref.shape, NEG_BIG, F32)
    acc_ref[...] = jnp.zeros(acc_ref.shape, F32)

    def scores(h, kb):
        rows = pl.ds(pl.multiple_of(kb * t, t), KEY_TILE)
        lanes = slice(h * HEAD_SLOT, (h + 1) * HEAD_SLOT)
        st = _dot(k_ref[rows, lanes], qt_ref[0, lanes, :])
        s_ref[h] = st
        bm_ref[h] = jnp.max(st, axis=0, keepdims=True)

    def accumulate(h, kb, masked):
        st = s_ref[h]
        if masked:
            key = kb * t + lax.broadcasted_iota(jnp.int32, (KEY_TILE, t), 0)
            qry = i * t + lax.broadcasted_iota(jnp.int32, (KEY_TILE, t), 1)
            st = jnp.where(key <= qry, st, NEG_BIG)
            bm = jnp.max(st, axis=0, keepdims=True)
        else:
            bm = bm_ref[h]
        m_old = m_ref[h]
        m_new = jnp.maximum(m_old, bm)
        p = jnp.exp2(st - m_new).astype(BF16)
        vrows = slice(h * V_SLOT, (h + 1) * V_SLOT)
        pv = _dot(vt_ref[kb, vrows, :], p[:t])
        for c in range(1, per_tile):
            pv = pv + _dot(vt_ref[kb + c, vrows, :], p[c * t:(c + 1) * t])
        acc_ref[h] = jnp.exp2(m_old - m_new) * acc_ref[h] + pv
        m_ref[h] = m_new

    full_tiles = i // per_tile
    scores(0, 0)

    def body(jt, carry):
        kb = jt * per_tile
        scores(1, kb)
        accumulate(0, kb, False)
        scores(0, kb + per_tile)
        accumulate(1, kb, False)
        return carry

    lax.fori_loop(0, full_tiles, body, 0)
    last = full_tiles * per_tile
    scores(1, last)
    accumulate(0, last, True)
    accumulate(1, last, True)

    outs = []
    for h in range(2):
        acc = acc_ref[h]
        outs.append(acc[:FOX_HEAD_DIM] / acc[FOX_HEAD_DIM:FOX_HEAD_DIM + 1])
    o_ref[...] = jnp.concatenate(outs, axis=0).T.astype(o_ref.dtype)


def _fox_attn(qt, k, vt):
    s = k.shape[0]
    t = ATT_BLOCK
    pair = 2 * HEAD_SLOT
    return pl.pallas_call(
        _fox_attn_kernel,
        grid=(FOX_HEADS // 2, s // t),
        in_specs=[pl.BlockSpec((1, pair, t), lambda hp, i: (i, hp, 0)),
                  pl.BlockSpec((s, pair), lambda hp, i: (0, hp)),
                  pl.BlockSpec((s // t, 2 * V_SLOT, t), lambda hp, i: (0, hp, 0))],
        out_specs=pl.BlockSpec((t, LANES), lambda hp, i: (i, hp)),
        out_shape=jax.ShapeDtypeStruct((s, D_MODEL), BF16),
        scratch_shapes=[pltpu.VMEM((2, 1, t), F32), pltpu.VMEM((2, V_SLOT, t), F32),
                        pltpu.VMEM((2, KEY_TILE, t), F32), pltpu.VMEM((2, 1, t), F32)],
        compiler_params=_params("parallel", "arbitrary"),
        name="fox_attn",
    )(qt, k, vt)


def _fox_layer(h, w_qkv, w_f, b_f, w_o, g, b):
    qt, k, vt = _fox_qkv(h, w_qkv, w_f, b_f)
    o = _fox_attn(qt, k, vt)
    return _proj_ln(h, o, w_o.astype(BF16), g, b)


def _s5_slab(gb, part, c):
    return (gb // 2) * 16 + part * 8 + (gb % 2) * 4 + c


def _s5_kernel(x_ref, win_ref, bm_ref, cm_ref, lam_ref, d_ref, wglu_ref, bglu_ref, g_ref, b_ref,
               o_ref, bu_ref, h_ref, state_ref, y_ref):
    n = S5_ROWS
    pitch = S5_PITCH
    stride = pitch - 1
    ngb = S5_GROUPS // S5_GB
    nvg = S5_SLABS // 16

    @pl.when(pl.program_id(0) == 0)
    def _():
        bu_ref[...] = jnp.zeros_like(bu_ref)
        state_ref[...] = jnp.zeros_like(state_ref)

    x = x_ref[...]
    u = _dot(x.astype(BF16), win_ref[...])
    for gb in range(ngb):
        bu = _dot(u[:, gb * S5_GB_CH:(gb + 1) * S5_GB_CH].astype(BF16), bm_ref[gb])
        for c in range(8):
            slab = _s5_slab(gb, c // 4, c % 4)
            bu_ref[pl.ds(slab * pitch + S5_HALO, n), :] = bu[:, c * LANES:(c + 1) * LANES]

    lam = [(lam_ref[0, v], lam_ref[1, v]) for v in range(nvg)]
    sub = lax.broadcasted_iota(jnp.int32, (8, LANES), 0)

    def step(k, state, valid):
        new_state = []
        for v in range(nvg):
            lr, li = lam[v]
            hr, hi = state[2 * v], state[2 * v + 1]
            rows_r = pl.ds(v * 16 * pitch + S5_HALO + k, 8, stride=stride)
            rows_i = pl.ds((v * 16 + 8) * pitch + S5_HALO + k, 8, stride=stride)
            nr = lr * hr - li * hi + bu_ref[rows_r, :]
            ni = lr * hi + li * hr + bu_ref[rows_i, :]
            if valid is not None:
                nr = jnp.where(valid, nr, hr)
                ni = jnp.where(valid, ni, hi)
            h_ref[rows_r, :] = nr
            h_ref[rows_i, :] = ni
            new_state += [nr, ni]
        return tuple(new_state)

    state = tuple(state_ref[i] for i in range(2 * nvg))
    for k in range(S5_SKEW):
        state = step(k, state, sub <= k)
    state = lax.fori_loop(S5_SKEW, n, lambda k, st: step(k, st, None), state, unroll=4)
    for k in range(n, n + S5_SKEW):
        state = step(k, state, sub > k - n)
    for i in range(2 * nvg):
        state_ref[i] = state[i]

    for gb in range(ngb):
        ch = slice(gb * S5_GB_CH, (gb + 1) * S5_GB_CH)
        parts = [h_ref[pl.ds(_s5_slab(gb, c // 4, c % 4) * pitch + S5_HALO, n), :]
                 for c in range(8)]
        hcat = jnp.concatenate(parts, axis=1).astype(BF16)
        y = _dot(hcat, cm_ref[gb]) + d_ref[:, ch] * u[:, ch]
        y_ref[:, ch] = jax.nn.gelu(y).astype(BF16)

    d = x.shape[1]
    z = _dot(y_ref[...], wglu_ref[...]) + bglu_ref[...]
    m = z[:, :d] * jax.nn.sigmoid(z[:, d:])
    o_ref[...] = _layer_norm(DN_ALPHA * x + m, g_ref[...], b_ref[...])


def _s5_tables(a_re, a_im, log_dt, b_re, b_im, c_re, c_im):
    dt = jnp.exp(log_dt)[:, None]
    mag = jnp.exp(a_re * dt)
    lr, li = mag * jnp.cos(a_im * dt), mag * jnp.sin(a_im * dt)
    den = a_re * a_re + a_im * a_im
    zr = ((lr - 1.0) * a_re + li * a_im) / den
    zi = (li * a_re - (lr - 1.0) * a_im) / den
    bbr = zr[..., None] * b_re - zi[..., None] * b_im
    bbi = zr[..., None] * b_im + zi[..., None] * b_re

    ngb = S5_GROUPS // S5_GB
    eye = jnp.eye(S5_GB, dtype=F32)

    def blockdiag_in(bb):
        t = bb.reshape(ngb, S5_GB, S5_STATE, S5_GROUP).transpose(0, 1, 3, 2)
        t = t[:, :, :, None, :] * eye[None, :, None, :, None]
        return t.reshape(ngb, S5_GB_CH, S5_GB_ST)

    def blockdiag_out(cc):
        t = cc.reshape(ngb, S5_GB, S5_GROUP, S5_STATE).transpose(0, 1, 3, 2)
        t = t[:, :, :, None, :] * eye[None, :, None, :, None]
        return t.reshape(ngb, S5_GB_ST, S5_GB_CH)

    bm = jnp.concatenate([blockdiag_in(bbr), blockdiag_in(bbi)], axis=2).astype(BF16)
    cm = jnp.concatenate([blockdiag_out(c_re), blockdiag_out(-c_im)], axis=1).astype(BF16)

    lam = jnp.stack([lr, li]).reshape(2, S5_SLABS // 16, 8, LANES)
    return bm, cm, lam


def _s5_layer(h, w_in, a_re, a_im, log_dt, b_re, b_im, c_re, c_im, d_skip, w_glu, b_glu, g, b):
    s, d = h.shape
    bm, cm, lam = _s5_tables(a_re, a_im, log_dt, b_re, b_im, c_re, c_im)
    n = S5_ROWS
    row = pl.BlockSpec((n, d), lambda i: (i, 0))
    slab_rows = S5_SLABS * S5_PITCH + S5_HALO
    return pl.pallas_call(
        _s5_kernel,
        grid=(s // n,),
        in_specs=[row, _resident((d, d)), _resident(bm.shape), _resident(cm.shape),
                  _resident(lam.shape), _resident((1, d)), _resident((d, 2 * d)),
                  _resident((1, 2 * d)), _resident((1, d)), _resident((1, d))],
        out_specs=row,
        out_shape=jax.ShapeDtypeStruct((s, d), F32),
        scratch_shapes=[pltpu.VMEM((slab_rows, LANES), F32), pltpu.VMEM((slab_rows, LANES), F32),
                        pltpu.VMEM((S5_SLABS // 8, 8, LANES), F32), pltpu.VMEM((n, d), BF16)],
        compiler_params=_params("arbitrary"),
        name="s5",
    )(h, w_in.astype(BF16), bm, cm, lam, d_skip.reshape(1, d), w_glu.astype(BF16),
      b_glu.reshape(1, 2 * d), g, b)


def _pool_ln_kernel(x_ref, win_ref, wgrp_ref, scale_ref, wout_ref, g_ref, b_ref, o_ref, ext_ref):
    i = pl.program_id(0)
    n = ROW_BLOCK

    @pl.when(i == 0)
    def _():
        ext_ref[0:POOL_HALO, :] = jnp.zeros((POOL_HALO, D_MODEL), F32)

    x = x_ref[...]
    u = _dot(x.astype(BF16), win_ref[...])
    ext_ref[POOL_HALO:, :] = u
    acc = ext_ref[...]
    pos1 = (i * n + 1 + lax.broadcasted_iota(jnp.int32, (n, 1), 0)).astype(F32)
    mixed = []
    for gi, w in enumerate(POOL_WINDOWS):
        acc = acc[:, (POOL_GROUP if gi else 0):]
        acc = acc + pltpu.roll(acc, w // 2, 0)
        cols = slice(gi * POOL_GROUP, (gi + 1) * POOL_GROUP)
        mean = acc[POOL_HALO:, :POOL_GROUP] / jnp.minimum(pos1, float(w))
        m = (mean - u[:, cols]).astype(BF16)
        mixed.append(_dot(m, wgrp_ref[gi]))
    m = (jnp.concatenate(mixed, axis=1) * scale_ref[...]).astype(BF16)
    y = _dot(m, wout_ref[...])
    o_ref[...] = _layer_norm(DN_ALPHA * x + y, g_ref[...], b_ref[...])
    ext_ref[0:POOL_HALO, :] = u[n - POOL_HALO:, :]


def _pool_layer(h, w_in, w_grp, scale, w_out, g, b):
    s, d = h.shape
    row = pl.BlockSpec((ROW_BLOCK, d), lambda i: (i, 0))
    return pl.pallas_call(
        _pool_ln_kernel,
        grid=(s // ROW_BLOCK,),
        in_specs=[row, _resident((d, d)), _resident(w_grp.shape), _resident((1, d)),
                  _resident((d, d)), _resident((1, d)), _resident((1, d))],
        out_specs=row,
        out_shape=jax.ShapeDtypeStruct((s, d), F32),
        scratch_shapes=[pltpu.VMEM((ROW_BLOCK + POOL_HALO, d), F32)],
        compiler_params=_params("arbitrary"),
        name="pool_ln",
    )(h, w_in.astype(BF16), w_grp.astype(BF16), scale.reshape(1, d), w_out.astype(BF16), g, b)


def kernel(x, ln_g, ln_b, fox_w_qkv, fox_w_f, fox_b_f, fox_w_o, s5_w_in, s5_a_re, s5_a_im, s5_log_dt, s5_b_re, s5_b_im, s5_c_re, s5_c_im, s5_d, s5_w_glu, s5_b_glu, pool_w_in, pool_w_grp, pool_scale, pool_w_out, ffn_w1, ffn_w3, ffn_w2):
    bsz, s, d = x.shape
    outs = []
    for bi in range(bsz):
        h = x[bi]
        for i in range(DEPTH):
            kind, j = i % N_MIXERS, i // N_MIXERS
            g0, b0 = ln_g[i, 0].reshape(1, d), ln_b[i, 0].reshape(1, d)
            g1, b1 = ln_g[i, 1].reshape(1, d), ln_b[i, 1].reshape(1, d)
            if kind == 0:
                h = _fox_layer(h, fox_w_qkv[j], fox_w_f[j], fox_b_f[j], fox_w_o[j], g0, b0)
            elif kind == 1:
                h = _s5_layer(h, s5_w_in[j], s5_a_re[j], s5_a_im[j], s5_log_dt[j],
                              s5_b_re[j], s5_b_im[j], s5_c_re[j], s5_c_im[j],
                              s5_d[j], s5_w_glu[j], s5_b_glu[j], g0, b0)
            else:
                h = _pool_layer(h, pool_w_in[j], pool_w_grp[j], pool_scale[j], pool_w_out[j], g0, b0)
            h = _ffn_ln(h, ffn_w1[i].astype(BF16), ffn_w3[i].astype(BF16), ffn_w2[i].astype(BF16),
                        g1, b1)
        outs.append(h)
    return jnp.stack(outs, axis=0)
```

```python
import math

import jax
import jax.numpy as jnp
import numpy as np
from jax import lax
from jax.experimental import pallas as pl
from jax.experimental.pallas import tpu as pltpu

F32 = jnp.float32
BF16 = jnp.bfloat16

D_MODEL = 1024
SEQ = 16384
DEPTH = 4
N_MIXERS = 3
FOX_HEADS = 16
FOX_HEAD_DIM = D_MODEL // FOX_HEADS
S5_GROUP = 16
S5_GROUPS = D_MODEL // S5_GROUP
S5_STATE = 64
POOL_WINDOWS = (2, 4, 8, 16)
POOL_GROUP = D_MODEL // len(POOL_WINDOWS)
D_FF = 2816
DN_ALPHA = (2.0 * DEPTH) ** 0.25
LN_EPS = 1e-5

LANES = 128
VMEM_LIMIT = 56 * 1024 * 1024

ROW_BLOCK = 512
FF_CHUNK = 256
ATT_BLOCK = 1024
KEY_CHUNK = 256
HEAD_SLOT = 2 * FOX_HEAD_DIM
V_SLOT = 80
LOG2E = math.log2(math.e)
S5_ROWS = 256
S5_SKEW = 7
S5_HALO = 8
S5_PITCH = S5_ROWS + S5_HALO
S5_SLABS = 64
S5_GB = 8
S5_GB_CH = S5_GB * S5_GROUP
S5_GB_ST = S5_GB * S5_STATE
POOL_HALO = 16
NEG_BIG = -1e30


def _resident(shape):
    nd = len(shape)
    return pl.BlockSpec(shape, lambda *_: (0,) * nd, pipeline_mode=pl.Buffered(1))


def _params(*sem):
    return pltpu.CompilerParams(dimension_semantics=sem, vmem_limit_bytes=VMEM_LIMIT)


def _dot(a, b):
    return jnp.dot(a, b, preferred_element_type=F32)


def _layer_norm(z, g, b):
    mu = jnp.mean(z, axis=-1, keepdims=True)
    zc = z - mu
    var = jnp.mean(zc * zc, axis=-1, keepdims=True)
    return zc * lax.rsqrt(var + LN_EPS) * g + b


def _split3(v):
    p0 = v.astype(BF16)
    r = v - p0.astype(F32)
    p1 = r.astype(BF16)
    p2 = (r - p1.astype(F32)).astype(BF16)
    return p0, p1, p2


def _proj_ln_kernel(x_ref, y_ref, w_ref, g_ref, b_ref, o_ref):
    m = _dot(y_ref[...], w_ref[...])
    o_ref[...] = _layer_norm(DN_ALPHA * x_ref[...] + m, g_ref[...], b_ref[...])


def _proj_ln(x, y, w, g, b):
    s, d = x.shape
    row = pl.BlockSpec((ROW_BLOCK, d), lambda i: (i, 0))
    return pl.pallas_call(
        _proj_ln_kernel,
        grid=(s // ROW_BLOCK,),
        in_specs=[row, row, _resident((d, d)), _resident((1, d)), _resident((1, d))],
        out_specs=row,
        out_shape=jax.ShapeDtypeStruct((s, d), F32),
        compiler_params=_params("parallel"),
        name="proj_ln",
    )(x, y, w, g, b)


def _ffn_ln_kernel(x_ref, w1_ref, w3_ref, w2_ref, g_ref, b_ref, o_ref):
    x = x_ref[...]
    xb = x.astype(BF16)
    acc = jnp.zeros(x.shape, F32)
    for c in range(D_FF // FF_CHUNK):
        cols = slice(c * FF_CHUNK, (c + 1) * FF_CHUNK)
        a = _dot(xb, w1_ref[:, cols])
        b = _dot(xb, w3_ref[:, cols])
        gate = (a * jax.nn.sigmoid(a) * b).astype(BF16)
        acc = acc + _dot(gate, w2_ref[cols, :])
    o_ref[...] = _layer_norm(DN_ALPHA * x + acc, g_ref[...], b_ref[...])


def _ffn_ln(x, w1, w3, w2, g, b):
    s, d = x.shape
    row = pl.BlockSpec((ROW_BLOCK, d), lambda i: (i, 0))
    return pl.pallas_call(
        _ffn_ln_kernel,
        grid=(s // ROW_BLOCK,),
        in_specs=[row, _resident((d, D_FF)), _resident((d, D_FF)), _resident((D_FF, d)),
                  _resident((1, d)), _resident((1, d))],
        out_specs=row,
        out_shape=jax.ShapeDtypeStruct((s, d), F32),
        compiler_params=_params("parallel"),
        name="ffn_ln",
    )(x, w1, w3, w2, g, b)


def _fox_qkv_kernel(x_ref, wqt_ref, eqt_ref, oneq_ref, wk_ref, ek_ref, onek_ref, wvt_ref, vone_ref,
                    wf_ref, bf_ref, tri_ref, qt_ref, k_ref, vt_ref, carry_ref):
    @pl.when(pl.program_id(0) == 0)
    def _():
        carry_ref[...] = jnp.zeros_like(carry_ref)

    nt = (((1,), (1,)), ((), ()))
    xb = x_ref[...].astype(BF16)
    z = _dot(xb, wf_ref[...]) + bf_ref[...]
    log_f = jnp.minimum(z, 0.0) - jnp.log1p(jnp.exp(-jnp.abs(z)))
    tri = tri_ref[...]
    p0, p1, p2 = _split3(log_f)
    c = _dot(tri, p0) + _dot(tri, p1) + _dot(tri, p2) + carry_ref[...]
    carry_ref[...] = c[ROW_BLOCK - 1:ROW_BLOCK, :]
    c3 = jnp.concatenate(_split3(c * LOG2E), axis=1)
    k_ref[...] = (_dot(xb, wk_ref[...]) + _dot(c3, ek_ref[...]) + onek_ref[...]).astype(BF16)
    qt = (lax.dot_general(wqt_ref[...], xb, nt, preferred_element_type=F32)
          + lax.dot_general(eqt_ref[...], c3, nt, preferred_element_type=F32))
    qt_ref[0] = (qt + oneq_ref[...]).astype(BF16)
    vt = lax.dot_general(wvt_ref[...], xb, nt, preferred_element_type=F32)
    vt_ref[0] = (vt + vone_ref[...]).astype(BF16)


def _fox_qkv(x, w_qkv, w_f, b_f):
    s, d = x.shape
    nq = FOX_HEADS * HEAD_SLOT
    nv = FOX_HEADS * V_SLOT
    dh = FOX_HEAD_DIM
    scale = dh ** -0.5 * LOG2E

    def slots(w):
        w = w.reshape(d, FOX_HEADS, dh)
        return jnp.pad(w, ((0, 0), (0, 0), (0, HEAD_SLOT - dh))).reshape(d, nq)

    wqt = slots(w_qkv[:, :d] * scale).T.astype(BF16)
    wk = slots(w_qkv[:, d:2 * d]).astype(BF16)
    wv = w_qkv[:, 2 * d:].T.reshape(FOX_HEADS, dh, d)
    wvt = jnp.pad(wv, ((0, 0), (0, V_SLOT - dh), (0, 0))).reshape(nv, d).astype(BF16)
    vone = np.zeros((FOX_HEADS, V_SLOT), np.float32)
    vone[:, dh] = 1.0
    vone = vone.reshape(nv, 1)
    wf =jnp.pad(w_f, ((0, 0), (0, LANES - FOX_HEADS))).astype(BF16)
    bf = jnp.pad(b_f, (0, LANES - FOX_HEADS)).reshape(1, LANES)

    heads = np.arange(FOX_HEADS)
    eq = np.zeros((3, LANES, nq), np.float32)
    ek = np.zeros((3, LANES, nq), np.float32)
    oneq = np.zeros((nq,), np.float32)
    onek = np.zeros((nq,), np.float32)
    for i in range(3):
        eq[i, heads, heads * HEAD_SLOT + dh + i] = 1.0
        ek[i, heads, heads * HEAD_SLOT + dh + 3 + i] = -1.0
        oneq[heads * HEAD_SLOT + dh + 3 + i] = 1.0
        onek[heads * HEAD_SLOT + dh + i] = 1.0
    eqt = eq.reshape(3 * LANES, nq).T.astype(BF16)
    ek = ek.reshape(3 * LANES, nq).astype(BF16)
    r = np.arange(ROW_BLOCK)
    tri = (r[:, None] >= r[None, :]).astype(BF16)

    nb = s // ROW_BLOCK
    return pl.pallas_call(
        _fox_qkv_kernel,
        grid=(nb,),
        in_specs=[pl.BlockSpec((ROW_BLOCK, d), lambda i: (i, 0)),
                  _resident((nq, d)), _resident((nq, 3 * LANES)), _resident((nq, 1)),
                  _resident((d, nq)), _resident((3 * LANES, nq)), _resident((1, nq)),
                  _resident((nv, d)), _resident((nv, 1)),
                  _resident((d, LANES)), _resident((1, LANES)),
                  _resident((ROW_BLOCK, ROW_BLOCK))],
        out_specs=[pl.BlockSpec((1, nq, ROW_BLOCK), lambda i: (i, 0, 0)),
                   pl.BlockSpec((ROW_BLOCK, nq), lambda i: (i, 0)),
                   pl.BlockSpec((1, nv, ROW_BLOCK), lambda i: (i, 0, 0))],
        out_shape=[jax.ShapeDtypeStruct((nb, nq, ROW_BLOCK), BF16),
                   jax.ShapeDtypeStruct((s, nq), BF16),
                   jax.ShapeDtypeStruct((nb, nv, ROW_BLOCK), BF16)],
        scratch_shapes=[pltpu.VMEM((1, LANES), F32)],
        compiler_params=_params("arbitrary"),
        name="fox_qkv",
    )(x, wqt, eqt, oneq.reshape(nq, 1), wk, ek, onek.reshape(1, nq), wvt, vone, wf, bf, tri)


def _fox_attn_kernel(qt_ref, k_ref, vt_ref, o_ref, m_ref, acc_ref, s0_ref, s1_ref, bm_ref):
    i = pl.program_id(1)
    t = ATT_BLOCK
    rb = ROW_BLOCK
    chunks = t // KEY_CHUNK
    s_refs = (s0_ref, s1_ref)
    m_ref[...] = jnp.full(m_ref.shape, NEG_BIG, F32)
    acc_ref[...] = jnp.zeros(acc_ref.shape, F32)

    def score_chunk(h, tile, c):
        rows = pl.ds(pl.multiple_of(tile * t + c * KEY_CHUNK, KEY_CHUNK), KEY_CHUNK)
        lanes = slice(h * HEAD_SLOT, (h + 1) * HEAD_SLOT)
        k = k_ref[rows, lanes]
        st = jnp.concatenate([_dot(k, qt_ref[b, lanes, :]) for b in range(t // rb)], axis=1)
        s_refs[h][c * KEY_CHUNK:(c + 1) * KEY_CHUNK, :] = st
        bm_ref[h, c] = jnp.max(st, axis=0, keepdims=True)

    def value_chunk(h, tile, c, m_new):
        p = jnp.exp2(s_refs[h][c * KEY_CHUNK:(c + 1) * KEY_CHUNK, :] - m_new).astype(BF16)
        blk, off = divmod(c * KEY_CHUNK, rb)
        vt = vt_ref[tile * (t // rb) + blk, h * V_SLOT:(h + 1) * V_SLOT, off:off + KEY_CHUNK]
        return _dot(vt, p)

    def step(h_acc, tile_acc, h_sc, tile_sc):
        m_old = m_ref[h_acc]
        m_new = m_old
        for c in range(chunks):
            m_new = jnp.maximum(m_new, bm_ref[h_acc, c])
        pv = None
        for c in range(chunks):
            if h_sc is not None:
                score_chunk(h_sc, tile_sc, c)
            contrib = value_chunk(h_acc, tile_acc, c, m_new)
            pv = contrib if pv is None else pv + contrib
        acc_ref[h_acc] = jnp.exp2(m_old - m_new) * acc_ref[h_acc] + pv
        m_ref[h_acc] = m_new

    def mask_tile(h):
        kc = KEY_CHUNK
        tri = (lax.broadcasted_iota(jnp.int32, (kc, kc), 0)
               <= lax.broadcasted_iota(jnp.int32, (kc, kc), 1))
        for c in range(chunks):
            rows = slice(c * kc, (c + 1) * kc)
            diag = slice(c * kc, (c + 1) * kc)
            d = jnp.where(tri, s_refs[h][rows, diag], NEG_BIG)
            s_refs[h][rows, diag] = d
            pieces = [jnp.max(d, axis=0, keepdims=True)]
            if c:
                s_refs[h][rows, :c * kc] = jnp.full((kc, c * kc), NEG_BIG, F32)
                pieces.insert(0, jnp.full((1, c * kc), NEG_BIG, F32))
            if c + 1 < chunks:
                pieces.append(bm_ref[h, c][:, (c + 1) * kc:])
            bm_ref[h, c] = jnp.concatenate(pieces, axis=1)

    for c in range(chunks):
        score_chunk(0, 0, c)

    def body(tile, carry):
        step(0, tile, 1, tile)
        step(1, tile, 0, tile + 1)
        return carry

    lax.fori_loop(0, i, body, 0)
    mask_tile(0)
    step(0, i, 1, i)
    mask_tile(1)
    step(1, i, None, None)

    outs = []
    for h in range(2):
        acc = acc_ref[h]
        outs.append(acc[:FOX_HEAD_DIM] / acc[FOX_HEAD_DIM:FOX_HEAD_DIM + 1])
    o_ref[...] = jnp.concatenate(outs, axis=0).T.astype(o_ref.dtype)


def _fox_attn(qt, k, vt):
    s = k.shape[0]
    t = ATT_BLOCK
    pair = 2 * HEAD_SLOT
    return pl.pallas_call(
        _fox_attn_kernel,
        grid=(FOX_HEADS // 2, s // t),
        in_specs=[pl.BlockSpec((t // ROW_BLOCK, pair, ROW_BLOCK), lambda hp, i: (i, hp, 0)),
                  pl.BlockSpec((s, pair), lambda hp, i: (0, hp)),
                  pl.BlockSpec((s // ROW_BLOCK, 2 * V_SLOT, ROW_BLOCK), lambda hp, i: (0, hp, 0))],
        out_specs=pl.BlockSpec((t, LANES), lambda hp, i: (i, hp)),
        out_shape=jax.ShapeDtypeStruct((s, D_MODEL), BF16),
        scratch_shapes=[pltpu.VMEM((2, 1, t), F32), pltpu.VMEM((2, V_SLOT, t), F32),
                        pltpu.VMEM((t, t), F32), pltpu.VMEM((t, t), F32),
                        pltpu.VMEM((2, t // KEY_CHUNK, 1, t), F32)],
        compiler_params=_params("parallel", "arbitrary"),
        name="fox_attn",
    )(qt, k, vt)


def _fox_layer(h, w_qkv, w_f, b_f, w_o, g, b):
    qt, k, vt = _fox_qkv(h, w_qkv, w_f, b_f)
    o = _fox_attn(qt, k, vt)
    return _proj_ln(h, o, w_o.astype(BF16), g, b)


def _s5_slab(gb, part, c):
    return (gb // 2) * 16 + part * 8 + (gb % 2) * 4 + c


def _s5_kernel(x_ref, win_ref, bm_ref, cm_ref, lam_ref, d_ref, wglu_ref, bglu_ref, g_ref, b_ref,
               o_ref, bu_ref, h_ref, state_ref, y_ref):
    n = S5_ROWS
    pitch = S5_PITCH
    stride = pitch - 1
    ngb = S5_GROUPS // S5_GB
    nvg = S5_SLABS // 16

    @pl.when(pl.program_id(0) == 0)
    def _():
        bu_ref[...] = jnp.zeros_like(bu_ref)
        state_ref[...] = jnp.zeros_like(state_ref)

    x = x_ref[...]
    u = _dot(x.astype(BF16), win_ref[...])
    for gb in range(ngb):
        bu = _dot(u[:, gb * S5_GB_CH:(gb + 1) * S5_GB_CH].astype(BF16), bm_ref[gb])
        for c in range(8):
            slab = _s5_slab(gb, c // 4, c % 4)
            bu_ref[pl.ds(slab * pitch + S5_HALO, n), :] = bu[:, c * LANES:(c + 1) * LANES]

    lam = [(lam_ref[0, v], lam_ref[1, v]) for v in range(nvg)]
    sub = lax.broadcasted_iota(jnp.int32, (8, LANES), 0)

    def step(k, state, valid):
        new_state = []
        for v in range(nvg):
            lr, li = lam[v]
            hr, hi = state[2 * v], state[2 * v + 1]
            rows_r = pl.ds(v * 16 * pitch + S5_HALO + k, 8, stride=stride)
            rows_i = pl.ds((v * 16 + 8) * pitch + S5_HALO + k, 8, stride=stride)
            nr = lr * hr - li * hi + bu_ref[rows_r, :]
            ni = lr * hi + li * hr + bu_ref[rows_i, :]
            if valid is not None:
                nr = jnp.where(valid, nr, hr)
                ni = jnp.where(valid, ni, hi)
            h_ref[rows_r, :] = nr
            h_ref[rows_i, :] = ni
            new_state += [nr, ni]
        return tuple(new_state)

    state = tuple(state_ref[i] for i in range(2 * nvg))
    for k in range(S5_SKEW):
        state = step(k, state, sub <= k)
    state = lax.fori_loop(S5_SKEW, n, lambda k, st: step(k, st, None), state, unroll=4)
    for k in range(n, n + S5_SKEW):
        state = step(k, state, sub > k - n)
    for i in range(2 * nvg):
        state_ref[i] = state[i]

    for gb in range(ngb):
        ch = slice(gb * S5_GB_CH, (gb + 1) * S5_GB_CH)
        parts = [h_ref[pl.ds(_s5_slab(gb, c // 4, c % 4) * pitch + S5_HALO, n), :]
                 for c in range(8)]
        hcat = jnp.concatenate(parts, axis=1).astype(BF16)
        y = _dot(hcat, cm_ref[gb]) + d_ref[:, ch] * u[:, ch]
        y_ref[:, ch] = jax.nn.gelu(y).astype(BF16)

    d = x.shape[1]
    z = _dot(y_ref[...], wglu_ref[...]) + bglu_ref[...]
    m = z[:, :d] * jax.nn.sigmoid(z[:, d:])
    o_ref[...] = _layer_norm(DN_ALPHA * x + m, g_ref[...], b_ref[...])


def _s5_tables(a_re, a_im, log_dt, b_re, b_im, c_re, c_im):
    dt = jnp.exp(log_dt)[:, None]
    mag = jnp.exp(a_re * dt)
    lr, li = mag * jnp.cos(a_im * dt), mag * jnp.sin(a_im * dt)
    den = a_re * a_re + a_im * a_im
    zr = ((lr - 1.0) * a_re + li * a_im) / den
    zi = (li * a_re - (lr - 1.0) * a_im) / den
    bbr = zr[..., None] * b_re - zi[..., None] * b_im
    bbi = zr[..., None] * b_im + zi[..., None] * b_re

    ngb = S5_GROUPS // S5_GB
    eye = np.eye(S5_GB, dtype=np.float32)

    def blockdiag_in(bb):
        t = bb.reshape(ngb, S5_GB, S5_STATE, S5_GROUP).transpose(0, 1, 3, 2)
        t = t[:, :, :, None, :] * eye[None, :, None, :, None]
        return t.reshape(ngb, S5_GB_CH, S5_GB_ST)

    def blockdiag_out(cc):
        t = cc.reshape(ngb, S5_GB, S5_GROUP, S5_STATE).transpose(0, 1, 3, 2)
        t = t[:, :, :, None, :] * eye[None, :, None, :, None]
        return t.reshape(ngb, S5_GB_ST, S5_GB_CH)

    bm = jnp.concatenate([blockdiag_in(bbr), blockdiag_in(bbi)], axis=2).astype(BF16)
    cm = jnp.concatenate([blockdiag_out(c_re), blockdiag_out(-c_im)], axis=1).astype(BF16)

    lam = jnp.stack([lr, li]).reshape(2, S5_SLABS // 16, 8, LANES)
    return bm, cm, lam


def _s5_layer(h, w_in, a_re, a_im, log_dt, b_re, b_im, c_re, c_im, d_skip, w_glu, b_glu, g, b):
    s, d = h.shape
    bm, cm, lam = _s5_tables(a_re, a_im, log_dt, b_re, b_im, c_re, c_im)
    n = S5_ROWS
    row = pl.BlockSpec((n, d), lambda i: (i, 0))
    slab_rows = S5_SLABS * S5_PITCH + S5_HALO
    return pl.pallas_call(
        _s5_kernel,
        grid=(s // n,),
        in_specs=[row, _resident((d, d)), _resident(bm.shape), _resident(cm.shape),
                  _resident(lam.shape), _resident((1, d)), _resident((d, 2 * d)),
                  _resident((1, 2 * d)), _resident((1, d)), _resident((1, d))],
        out_specs=row,
        out_shape=jax.ShapeDtypeStruct((s, d), F32),
        scratch_shapes=[pltpu.VMEM((slab_rows, LANES), F32), pltpu.VMEM((slab_rows, LANES), F32),
                        pltpu.VMEM((S5_SLABS // 8, 8, LANES), F32), pltpu.VMEM((n, d), BF16)],
        compiler_params=_params("arbitrary"),
        name="s5",
    )(h, w_in.astype(BF16), bm, cm, lam, d_skip.reshape(1, d), w_glu.astype(BF16),
      b_glu.reshape(1, 2 * d), g, b)


def _pool_ln_kernel(x_ref, win_ref, wgrp_ref, scale_ref, wout_ref, g_ref, b_ref, o_ref, ext_ref):
    i = pl.program_id(0)
    n = ROW_BLOCK

    @pl.when(i == 0)
    def _():
        ext_ref[0:POOL_HALO, :] = jnp.zeros((POOL_HALO, D_MODEL), F32)

    x = x_ref[...]
    u = _dot(x.astype(BF16), win_ref[...])
    ext_ref[POOL_HALO:, :] = u
    acc = ext_ref[...]
    pos1 = (i * n + 1 + lax.broadcasted_iota(jnp.int32, (n, 1), 0)).astype(F32)
    mixed = []
    for gi, w in enumerate(POOL_WINDOWS):
        acc = acc[:, (POOL_GROUP if gi else 0):]
        acc = acc + pltpu.roll(acc, w // 2, 0)
        cols = slice(gi * POOL_GROUP, (gi + 1) * POOL_GROUP)
        mean = acc[POOL_HALO:, :POOL_GROUP] / jnp.minimum(pos1, float(w))
        m = (mean - u[:, cols]).astype(BF16)
        mixed.append(_dot(m, wgrp_ref[gi]))
    m = (jnp.concatenate(mixed, axis=1) * scale_ref[...]).astype(BF16)
    y = _dot(m, wout_ref[...])
    o_ref[...] = _layer_norm(DN_ALPHA * x + y, g_ref[...], b_ref[...])
    ext_ref[0:POOL_HALO, :] = u[n - POOL_HALO:, :]


def _pool_layer(h, w_in, w_grp, scale, w_out, g, b):
    s, d = h.shape
    row = pl.BlockSpec((ROW_BLOCK, d), lambda i: (i, 0))
    return pl.pallas_call(
        _pool_ln_kernel,
        grid=(s // ROW_BLOCK,),
        in_specs=[row, _resident((d, d)), _resident(w_grp.shape), _resident((1, d)),
                  _resident((d, d)), _resident((1, d)), _resident((1, d))],
        out_specs=row,
        out_shape=jax.ShapeDtypeStruct((s, d), F32),
        scratch_shapes=[pltpu.VMEM((ROW_BLOCK + POOL_HALO, d), F32)],
        compiler_params=_params("arbitrary"),
        name="pool_ln",
    )(h, w_in.astype(BF16), w_grp.astype(BF16), scale.reshape(1, d), w_out.astype(BF16), g, b)


def kernel(x, ln_g, ln_b, fox_w_qkv, fox_w_f, fox_b_f, fox_w_o, s5_w_in, s5_a_re, s5_a_im, s5_log_dt, s5_b_re, s5_b_im, s5_c_re, s5_c_im, s5_d, s5_w_glu, s5_b_glu, pool_w_in, pool_w_grp, pool_scale, pool_w_out, ffn_w1, ffn_w3, ffn_w2):
    bsz, s, d = x.shape
    outs = []
    for bi in range(bsz):
        h = x[bi]
        for i in range(DEPTH):
            kind, j = i % N_MIXERS, i // N_MIXERS
            g0, b0 = ln_g[i, 0].reshape(1, d), ln_b[i, 0].reshape(1, d)
            g1, b1 = ln_g[i, 1].reshape(1, d), ln_b[i, 1].reshape(1, d)
            if kind == 0:
                h = _fox_layer(h, fox_w_qkv[j], fox_w_f[j], fox_b_f[j], fox_w_o[j], g0, b0)
            elif kind == 1:
                h = _s5_layer(h, s5_w_in[j], s5_a_re[j], s5_a_im[j], s5_log_dt[j],
                              s5_b_re[j], s5_b_im[j], s5_c_re[j], s5_c_im[j],
                              s5_d[j], s5_w_glu[j], s5_b_glu[j], g0, b0)
            else:
                h = _pool_layer(h, pool_w_in[j], pool_w_grp[j], pool_scale[j], pool_w_out[j], g0, b0)
            h = _ffn_ln(h, ffn_w1[i].astype(BF16), ffn_w3[i].astype(BF16), ffn_w2[i].astype(BF16),
                        g1, b1)
        outs.append(h)
    return jnp.stack(outs, axis=0)
```

```python
import math

import jax
import jax.numpy as jnp
import numpy as np
from jax import lax
from jax.experimental import pallas as pl
from jax.experimental.pallas import tpu as pltpu

F32 = jnp.float32
BF16 = jnp.bfloat16

D_MODEL = 1024
SEQ = 16384
DEPTH = 4
N_MIXERS = 3
FOX_HEADS = 16
FOX_HEAD_DIM = D_MODEL // FOX_HEADS
S5_GROUP = 16
S5_GROUPS = D_MODEL // S5_GROUP
S5_STATE = 64
POOL_WINDOWS = (2, 4, 8, 16)
POOL_GROUP = D_MODEL // len(POOL_WINDOWS)
D_FF = 2816
DN_ALPHA = (2.0 * DEPTH) ** 0.25
LN_EPS = 1e-5

LANES = 128
VMEM_LIMIT = 56 * 1024 * 1024

ROW_BLOCK = 512
FF_CHUNK = 256
ATT_BLOCK = 1024
KEY_CHUNK = 256
HEAD_SLOT = 2 * FOX_HEAD_DIM
V_SLOT = 80
LOG2E = math.log2(math.e)
S5_ROWS = 256
S5_SKEW = 7
S5_HALO = 8
S5_PITCH = S5_ROWS + S5_HALO
S5_SLABS = 64
S5_GB = 8
S5_GB_CH = S5_GB * S5_GROUP
S5_GB_ST = S5_GB * S5_STATE
POOL_HALO = 16
NEG_BIG = -1e30


def _resident(shape):
    nd = len(shape)
    return pl.BlockSpec(shape, lambda *_: (0,) * nd, pipeline_mode=pl.Buffered(1))


def _params(*sem):
    return pltpu.CompilerParams(dimension_semantics=sem, vmem_limit_bytes=VMEM_LIMIT)


def _dot(a, b):
    return jnp.dot(a, b, preferred_element_type=F32)


def _layer_norm(z, g, b):
    mu = jnp.mean(z, axis=-1, keepdims=True)
    zc = z - mu
    var = jnp.mean(zc * zc, axis=-1, keepdims=True)
    return zc * lax.rsqrt(var + LN_EPS) * g + b


def _split3(v):
    p0 = v.astype(BF16)
    r = v - p0.astype(F32)
    p1 = r.astype(BF16)
    p2 = (r - p1.astype(F32)).astype(BF16)
    return p0, p1, p2


def _ffn_block(x, w1_ref, w3_ref, w2_ref, g_ref, b_ref):
    xb = x.astype(BF16)
    acc = jnp.zeros(x.shape, F32)
    for c in range(D_FF // FF_CHUNK):
        cols = slice(c * FF_CHUNK, (c + 1) * FF_CHUNK)
        a = _dot(xb, w1_ref[:, cols])
        b = _dot(xb, w3_ref[:, cols])
        gate = (a * jax.nn.sigmoid(a) * b).astype(BF16)
        acc = acc + _dot(gate, w2_ref[cols, :])
    return _layer_norm(DN_ALPHA * x + acc, g_ref[...], b_ref[...])


def _ffn_specs(d):
    return [_resident((d, D_FF)), _resident((d, D_FF)), _resident((D_FF, d)),
            _resident((1, d)), _resident((1, d))]


def _ffn_ln_kernel(x_ref, w1_ref, w3_ref, w2_ref, g_ref, b_ref, o_ref):
    o_ref[...] = _ffn_block(x_ref[...], w1_ref, w3_ref, w2_ref, g_ref, b_ref)


def _ffn_ln(x, ffn):
    s, d = x.shape
    row = pl.BlockSpec((ROW_BLOCK, d), lambda i: (i, 0))
    return pl.pallas_call(
        _ffn_ln_kernel,
        grid=(s // ROW_BLOCK,),
        in_specs=[row] + _ffn_specs(d),
        out_specs=row,
        out_shape=jax.ShapeDtypeStruct((s, d), F32),
        compiler_params=_params("parallel"),
        name="ffn_ln",
    )(x, *ffn)


def _proj_ffn_kernel(x_ref, y_ref, w_ref, g_ref, b_ref, w1_ref, w3_ref, w2_ref, g1_ref, b1_ref,
                     o_ref):
    m = _dot(y_ref[...], w_ref[...])
    h = _layer_norm(DN_ALPHA * x_ref[...] + m, g_ref[...], b_ref[...])
    o_ref[...] = _ffn_block(h, w1_ref, w3_ref, w2_ref, g1_ref, b1_ref)


def _proj_ffn(x, y, w, g, b, ffn):
    s, d = x.shape
    row = pl.BlockSpec((ROW_BLOCK, d), lambda i: (i, 0))
    return pl.pallas_call(
        _proj_ffn_kernel,
        grid=(s // ROW_BLOCK,),
        in_specs=[row, row, _resident((d, d)), _resident((1, d)), _resident((1, d))]
        + _ffn_specs(d),
        out_specs=row,
        out_shape=jax.ShapeDtypeStruct((s, d), F32),
        compiler_params=_params("parallel"),
        name="proj_ffn",
    )(x, y, w, g, b, *ffn)


def _fox_qkv_kernel(x_ref, wqt_ref, eqt_ref, oneq_ref, wk_ref, ek_ref, onek_ref, wvt_ref, vone_ref,
                    wf_ref, bf_ref, tri_ref, qt_ref, k_ref, vt_ref, carry_ref):
    @pl.when(pl.program_id(0) == 0)
    def _():
        carry_ref[...] = jnp.zeros_like(carry_ref)

    nt = (((1,), (1,)), ((), ()))
    xb = x_ref[...].astype(BF16)
    z = _dot(xb, wf_ref[...]) + bf_ref[...]
    log_f = jnp.minimum(z, 0.0) - jnp.log1p(jnp.exp(-jnp.abs(z)))
    tri = tri_ref[...]
    p0, p1, p2 = _split3(log_f)
    c = _dot(tri, p0) + _dot(tri, p1) + _dot(tri, p2) + carry_ref[...]
    carry_ref[...] = c[ROW_BLOCK - 1:ROW_BLOCK, :]
    pieces = [p.astype(F32) for p in _split3(c * LOG2E)]
    lane = lax.broadcasted_iota(jnp.int32, c.shape, 1)
    c3 = jnp.zeros(c.shape, F32)
    for i in reversed(range(3)):
        shifted = pltpu.roll(pieces[i], i * FOX_HEADS, 1) if i else pieces[i]
        c3 = jnp.where(lane < (i + 1) * FOX_HEADS, shifted, c3)
    c3 = c3.astype(BF16)
    dh = FOX_HEAD_DIM
    k_main = _dot(xb, wk_ref[...])
    k_aug = _dot(c3, ek_ref[...]) + onek_ref[...]
    qt_main = lax.dot_general(wqt_ref[...], xb, nt, preferred_element_type=F32)
    qt_aug = lax.dot_general(eqt_ref[...], c3, nt, preferred_element_type=F32) + oneq_ref[...]
    low = lax.broadcasted_iota(jnp.int32, (ROW_BLOCK, HEAD_SLOT), 1) < dh
    for h in range(FOX_HEADS):
        slot = slice(h * HEAD_SLOT, (h + 1) * HEAD_SLOT)
        pair = k_main[:, (h // 2) * HEAD_SLOT:(h // 2 + 1) * HEAD_SLOT]
        mine = low if h % 2 == 0 else jnp.logical_not(low)
        k_ref[:, slot] = (jnp.where(mine, pair, 0.0) + k_aug[:, slot]).astype(BF16)
        main_rows = slice(h * HEAD_SLOT + (h % 2) * dh, h * HEAD_SLOT + (h % 2) * dh + dh)
        aug_rows = slice(h * HEAD_SLOT + (1 - h % 2) * dh, h * HEAD_SLOT + (1 - h % 2) * dh + dh)
        qt_ref[0, main_rows, :] = (qt_main[h * dh:(h + 1) * dh] + qt_aug[main_rows]).astype(BF16)
        qt_ref[0, aug_rows, :] = qt_aug[aug_rows].astype(BF16)
    vt = lax.dot_general(wvt_ref[...], xb, nt, preferred_element_type=F32)
    vt_ref[0] = (vt + vone_ref[...]).astype(BF16)


def _fox_qkv(x, w_qkv, w_f, b_f):
    s, d = x.shape
    nq = FOX_HEADS * HEAD_SLOT
    nv = FOX_HEADS * V_SLOT
    dh = FOX_HEAD_DIM
    scale = dh ** -0.5 * LOG2E

    wqt = (w_qkv[:, :d] * scale).T.astype(BF16)
    wk = w_qkv[:, d:2 * d].astype(BF16)
    wv = w_qkv[:, 2 * d:].T.reshape(FOX_HEADS, dh, d)
    wvt = jnp.pad(wv, ((0, 0), (0, V_SLOT - dh), (0, 0))).reshape(nv, d).astype(BF16)
    vone = np.zeros((FOX_HEADS, V_SLOT), np.float32)
    vone[:, dh] = 1.0
    vone = vone.reshape(nv, 1)
    wf =jnp.pad(w_f, ((0, 0), (0, LANES - FOX_HEADS))).astype(BF16)
    bf = jnp.pad(b_f, (0, LANES - FOX_HEADS)).reshape(1, LANES)

    heads = np.arange(FOX_HEADS)
    aug = heads * HEAD_SLOT + np.where(heads % 2 == 0, dh, 0)
    eq = np.zeros((LANES, nq), np.float32)
    ek = np.zeros((LANES, nq), np.float32)
    oneq = np.zeros((nq,), np.float32)
    onek = np.zeros((nq,), np.float32)
    for i in range(3):
        eq[i * FOX_HEADS + heads, aug + i] = 1.0
        ek[i * FOX_HEADS + heads, aug + 3 + i] = -1.0
        oneq[aug + 3 + i] = 1.0
        onek[aug + i] = 1.0
    eqt = eq.T.astype(BF16)
    ek = ek.astype(BF16)
    r = np.arange(ROW_BLOCK)
    tri = (r[:, None] >= r[None, :]).astype(BF16)

    nb = s // ROW_BLOCK
    return pl.pallas_call(
        _fox_qkv_kernel,
        grid=(nb,),
        in_specs=[pl.BlockSpec((ROW_BLOCK, d), lambda i: (i, 0)),
                  _resident((d, d)), _resident((nq, LANES)), _resident((nq, 1)),
                  _resident((d, d)), _resident((LANES, nq)), _resident((1, nq)),
                  _resident((nv, d)), _resident((nv, 1)),
                  _resident((d, LANES)), _resident((1, LANES)),
                  _resident((ROW_BLOCK, ROW_BLOCK))],
        out_specs=[pl.BlockSpec((1, nq, ROW_BLOCK), lambda i: (i, 0, 0)),
                   pl.BlockSpec((ROW_BLOCK, nq), lambda i: (i, 0)),
                   pl.BlockSpec((1, nv, ROW_BLOCK), lambda i: (i, 0, 0))],
        out_shape=[jax.ShapeDtypeStruct((nb, nq, ROW_BLOCK), BF16),
                   jax.ShapeDtypeStruct((s, nq), BF16),
                   jax.ShapeDtypeStruct((nb, nv, ROW_BLOCK), BF16)],
        scratch_shapes=[pltpu.VMEM((1, LANES), F32)],
        compiler_params=_params("arbitrary"),
        name="fox_qkv",
    )(x, wqt, eqt, oneq.reshape(nq, 1), wk, ek, onek.reshape(1, nq), wvt, vone, wf, bf, tri)


def _fox_attn_kernel(qt_ref, k_ref, vt_ref, o_ref, m_ref, acc_ref, s0_ref, s1_ref, bm_ref):
    i = pl.program_id(1)
    t = ATT_BLOCK
    rb = ROW_BLOCK
    chunks = t // KEY_CHUNK
    s_refs = (s0_ref, s1_ref)
    m_ref[...] = jnp.full(m_ref.shape, NEG_BIG, F32)
    acc_ref[...] = jnp.zeros(acc_ref.shape, F32)

    def score_chunk(h, tile, c):
        rows = pl.ds(pl.multiple_of(tile * t + c * KEY_CHUNK, KEY_CHUNK), KEY_CHUNK)
        lanes = slice(h * HEAD_SLOT, (h + 1) * HEAD_SLOT)
        k = k_ref[rows, lanes]
        st = jnp.concatenate([_dot(k, qt_ref[b, lanes, :]) for b in range(t // rb)], axis=1)
        s_refs[h][c * KEY_CHUNK:(c + 1) * KEY_CHUNK, :] = st
        bm_ref[h, c] = jnp.max(st, axis=0, keepdims=True)

    def value_chunk(h, tile, c, m_new):
        p = jnp.exp2(s_refs[h][c * KEY_CHUNK:(c + 1) * KEY_CHUNK, :] - m_new).astype(BF16)
        blk, off = divmod(c * KEY_CHUNK, rb)
        vt = vt_ref[tile * (t // rb) + blk, h * V_SLOT:(h + 1) * V_SLOT, off:off + KEY_CHUNK]
        return _dot(vt, p)

    def step(h_acc, tile_acc, h_sc, tile_sc):
        m_old = m_ref[h_acc]
        m_new = m_old
        for c in range(chunks):
            m_new = jnp.maximum(m_new, bm_ref[h_acc, c])
        pv = None
        for c in range(chunks):
            if h_sc is not None:
                score_chunk(h_sc, tile_sc, c)
            contrib = value_chunk(h_acc, tile_acc, c, m_new)
            pv = contrib if pv is None else pv + contrib
        acc_ref[h_acc] = jnp.exp2(m_old - m_new) * acc_ref[h_acc] + pv
        m_ref[h_acc] = m_new

    def mask_tile(h):
        kc = KEY_CHUNK
        tri = (lax.broadcasted_iota(jnp.int32, (kc, kc), 0)
               <= lax.broadcasted_iota(jnp.int32, (kc, kc), 1))
        for c in range(chunks):
            rows = slice(c * kc, (c + 1) * kc)
            diag = slice(c * kc, (c + 1) * kc)
            d = jnp.where(tri, s_refs[h][rows, diag], NEG_BIG)
            s_refs[h][rows, diag] = d
            pieces = [jnp.max(d, axis=0, keepdims=True)]
            if c:
                s_refs[h][rows, :c * kc] = jnp.full((kc, c * kc), NEG_BIG, F32)
                pieces.insert(0, jnp.full((1, c * kc), NEG_BIG, F32))
            if c + 1 < chunks:
                pieces.append(bm_ref[h, c][:, (c + 1) * kc:])
            bm_ref[h, c] = jnp.concatenate(pieces, axis=1)

    for c in range(chunks):
        score_chunk(0, 0, c)

    def body(tile, carry):
        step(0, tile, 1, tile)
        step(1, tile, 0, tile + 1)
        return carry

    lax.fori_loop(0, i, body, 0)
    mask_tile(0)
    step(0, i, 1, i)
    mask_tile(1)
    step(1, i, None, None)

    outs = []
    for h in range(2):
        acc = acc_ref[h]
        outs.append(acc[:FOX_HEAD_DIM] / acc[FOX_HEAD_DIM:FOX_HEAD_DIM + 1])
    o_ref[...] = jnp.concatenate(outs, axis=0).T.astype(o_ref.dtype)


def _fox_attn(qt, k, vt):
    s = k.shape[0]
    t = ATT_BLOCK
    pair = 2 * HEAD_SLOT
    return pl.pallas_call(
        _fox_attn_kernel,
        grid=(FOX_HEADS // 2, s // t),
        in_specs=[pl.BlockSpec((t // ROW_BLOCK, pair, ROW_BLOCK), lambda hp, i: (i, hp, 0)),
                  pl.BlockSpec((s, pair), lambda hp, i: (0, hp)),
                  pl.BlockSpec((s // ROW_BLOCK, 2 * V_SLOT, ROW_BLOCK), lambda hp, i: (0, hp, 0))],
        out_specs=pl.BlockSpec((t, LANES), lambda hp, i: (i, hp)),
        out_shape=jax.ShapeDtypeStruct((s, D_MODEL), BF16),
        scratch_shapes=[pltpu.VMEM((2, 1, t), F32), pltpu.VMEM((2, V_SLOT, t), F32),
                        pltpu.VMEM((t, t), F32), pltpu.VMEM((t, t), F32),
                        pltpu.VMEM((2, t // KEY_CHUNK, 1, t), F32)],
        compiler_params=_params("parallel", "arbitrary"),
        name="fox_attn",
    )(qt, k, vt)


def _fox_layer(h, w_qkv, w_f, b_f, w_o, g, b, ffn):
    qt, k, vt = _fox_qkv(h, w_qkv, w_f, b_f)
    o = _fox_attn(qt, k, vt)
    return _proj_ffn(h, o, w_o.astype(BF16), g, b, ffn)


def _s5_slab(gb, part, c):
    return (gb // 2) * 16 + part * 8 + (gb % 2) * 4 + c


def _s5_kernel(x_ref, win_ref, bm_ref, cm_ref, lam_ref, d_ref, wglu_ref, bglu_ref, g_ref, b_ref,
               o_ref, bu_ref, h_ref, state_ref, y_ref):
    n = S5_ROWS
    pitch = S5_PITCH
    stride = pitch - 1
    ngb = S5_GROUPS // S5_GB
    nvg = S5_SLABS // 16

    @pl.when(pl.program_id(0) == 0)
    def _():
        bu_ref[...] = jnp.zeros_like(bu_ref)
        state_ref[...] = jnp.zeros_like(state_ref)

    x = x_ref[...]
    u = _dot(x.astype(BF16), win_ref[...])
    for gb in range(ngb):
        bu = _dot(u[:, gb * S5_GB_CH:(gb + 1) * S5_GB_CH].astype(BF16), bm_ref[gb])
        for c in range(8):
            slab = _s5_slab(gb, c // 4, c % 4)
            bu_ref[pl.ds(slab * pitch + S5_HALO, n), :] = bu[:, c * LANES:(c + 1) * LANES]

    lam = [(lam_ref[0, v], lam_ref[1, v]) for v in range(nvg)]
    sub = lax.broadcasted_iota(jnp.int32, (8, LANES), 0)

    def step(k, state, valid):
        new_state = []
        for v in range(nvg):
            lr, li = lam[v]
            hr, hi = state[2 * v], state[2 * v + 1]
            rows_r = pl.ds(v * 16 * pitch + S5_HALO + k, 8, stride=stride)
            rows_i = pl.ds((v * 16 + 8) * pitch + S5_HALO + k, 8, stride=stride)
            nr = lr * hr - li * hi + bu_ref[rows_r, :]
            ni = lr * hi + li * hr + bu_ref[rows_i, :]
            if valid is not None:
                nr = jnp.where(valid, nr, hr)
                ni = jnp.where(valid, ni, hi)
            h_ref[rows_r, :] = nr
            h_ref[rows_i, :] = ni
            new_state += [nr, ni]
        return tuple(new_state)

    state = tuple(state_ref[i] for i in range(2 * nvg))
    for k in range(S5_SKEW):
        state = step(k, state, sub <= k)
    state = lax.fori_loop(S5_SKEW, n, lambda k, st: step(k, st, None), state, unroll=4)
    for k in range(n, n + S5_SKEW):
        state = step(k, state, sub > k - n)
    for i in range(2 * nvg):
        state_ref[i] = state[i]

    for gb in range(ngb):
        ch = slice(gb * S5_GB_CH, (gb + 1) * S5_GB_CH)
        parts = [h_ref[pl.ds(_s5_slab(gb, c // 4, c % 4) * pitch + S5_HALO, n), :]
                 for c in range(8)]
        hcat = jnp.concatenate(parts, axis=1).astype(BF16)
        y = _dot(hcat, cm_ref[gb]) + d_ref[:, ch] * u[:, ch]
        y_ref[:, ch] = jax.nn.gelu(y).astype(BF16)

    d = x.shape[1]
    z = _dot(y_ref[...], wglu_ref[...]) + bglu_ref[...]
    m = z[:, :d] * jax.nn.sigmoid(z[:, d:])
    o_ref[...] = _layer_norm(DN_ALPHA * x + m, g_ref[...], b_ref[...])


def _s5_tables(a_re, a_im, log_dt, b_re, b_im, c_re, c_im):
    dt = jnp.exp(log_dt)[:, None]
    mag = jnp.exp(a_re * dt)
    lr, li = mag * jnp.cos(a_im * dt), mag * jnp.sin(a_im * dt)
    den = a_re * a_re + a_im * a_im
    zr = ((lr - 1.0) * a_re + li * a_im) / den
    zi = (li * a_re - (lr - 1.0) * a_im) / den
    bbr = zr[..., None] * b_re - zi[..., None] * b_im
    bbi = zr[..., None] * b_im + zi[..., None] * b_re

    ngb = S5_GROUPS // S5_GB
    eye = np.eye(S5_GB, dtype=np.float32)

    def blockdiag_in(bb):
        t = bb.reshape(ngb, S5_GB, S5_STATE, S5_GROUP).transpose(0, 1, 3, 2)
        t = t[:, :, :, None, :] * eye[None, :, None, :, None]
        return t.reshape(ngb, S5_GB_CH, S5_GB_ST)

    def blockdiag_out(cc):
        t = cc.reshape(ngb, S5_GB, S5_GROUP, S5_STATE).transpose(0, 1, 3, 2)
        t = t[:, :, :, None, :] * eye[None, :, None, :, None]
        return t.reshape(ngb, S5_GB_ST, S5_GB_CH)

    bm = jnp.concatenate([blockdiag_in(bbr), blockdiag_in(bbi)], axis=2).astype(BF16)
    cm = jnp.concatenate([blockdiag_out(c_re), blockdiag_out(-c_im)], axis=1).astype(BF16)

    lam = jnp.stack([lr, li]).reshape(2, S5_SLABS // 16, 8, LANES)
    return bm, cm, lam


def _s5_layer(h, w_in, a_re, a_im, log_dt, b_re, b_im, c_re, c_im, d_skip, w_glu, b_glu, g, b):
    s, d = h.shape
    bm, cm, lam = _s5_tables(a_re, a_im, log_dt, b_re, b_im, c_re, c_im)
    n = S5_ROWS
    row = pl.BlockSpec((n, d), lambda i: (i, 0))
    slab_rows = S5_SLABS * S5_PITCH + S5_HALO
    return pl.pallas_call(
        _s5_kernel,
        grid=(s // n,),
        in_specs=[row, _resident((d, d)), _resident(bm.shape), _resident(cm.shape),
                  _resident(lam.shape), _resident((1, d)), _resident((d, 2 * d)),
                  _resident((1, 2 * d)), _resident((1, d)), _resident((1, d))],
        out_specs=row,
        out_shape=jax.ShapeDtypeStruct((s, d), F32),
        scratch_shapes=[pltpu.VMEM((slab_rows, LANES), F32), pltpu.VMEM((slab_rows, LANES), F32),
                        pltpu.VMEM((S5_SLABS // 8, 8, LANES), F32), pltpu.VMEM((n, d), BF16)],
        compiler_params=_params("arbitrary"),
        name="s5",
    )(h, w_in.astype(BF16), bm, cm, lam, d_skip.reshape(1, d), w_glu.astype(BF16),
      b_glu.reshape(1, 2 * d), g, b)


def _pool_ffn_kernel(x_ref, win_ref, wgrp_ref, scale_ref, wout_ref, g_ref, b_ref,
                     w1_ref, w3_ref, w2_ref, g1_ref, b1_ref, o_ref, ext_ref):
    i = pl.program_id(0)
    n = ROW_BLOCK

    @pl.when(i == 0)
    def _():
        ext_ref[0:POOL_HALO, :] = jnp.zeros((POOL_HALO, D_MODEL), F32)

    x = x_ref[...]
    u = _dot(x.astype(BF16), win_ref[...])
    ext_ref[POOL_HALO:, :] = u
    acc = ext_ref[...]
    pos1 = (i * n + 1 + lax.broadcasted_iota(jnp.int32, (n, 1), 0)).astype(F32)
    mixed = []
    for gi, w in enumerate(POOL_WINDOWS):
        acc = acc[:, (POOL_GROUP if gi else 0):]
        acc = acc + pltpu.roll(acc, w // 2, 0)
        cols = slice(gi * POOL_GROUP, (gi + 1) * POOL_GROUP)
        mean = acc[POOL_HALO:, :POOL_GROUP] / jnp.minimum(pos1, float(w))
        m = (mean - u[:, cols]).astype(BF16)
        mixed.append(_dot(m, wgrp_ref[gi]))
    m = (jnp.concatenate(mixed, axis=1) * scale_ref[...]).astype(BF16)
    y = _dot(m, wout_ref[...])
    ext_ref[0:POOL_HALO, :] = u[n - POOL_HALO:, :]
    h = _layer_norm(DN_ALPHA * x + y, g_ref[...], b_ref[...])
    o_ref[...] = _ffn_block(h, w1_ref, w3_ref, w2_ref, g1_ref, b1_ref)


def _pool_layer(h, w_in, w_grp, scale, w_out, g, b, ffn):
    s, d = h.shape
    row = pl.BlockSpec((ROW_BLOCK, d), lambda i: (i, 0))
    return pl.pallas_call(
        _pool_ffn_kernel,
        grid=(s // ROW_BLOCK,),
        in_specs=[row, _resident((d, d)), _resident(w_grp.shape), _resident((1, d)),
                  _resident((d, d)), _resident((1, d)), _resident((1, d))] + _ffn_specs(d),
        out_specs=row,
        out_shape=jax.ShapeDtypeStruct((s, d), F32),
        scratch_shapes=[pltpu.VMEM((ROW_BLOCK + POOL_HALO, d), F32)],
        compiler_params=_params("arbitrary"),
        name="pool_ffn",
    )(h, w_in.astype(BF16), w_grp.astype(BF16), scale.reshape(1, d), w_out.astype(BF16), g, b,
      *ffn)


def kernel(x, ln_g, ln_b, fox_w_qkv, fox_w_f, fox_b_f, fox_w_o, s5_w_in, s5_a_re, s5_a_im, s5_log_dt, s5_b_re, s5_b_im, s5_c_re, s5_c_im, s5_d, s5_w_glu, s5_b_glu, pool_w_in, pool_w_grp, pool_scale, pool_w_out, ffn_w1, ffn_w3, ffn_w2):
    bsz, s, d = x.shape
    outs = []
    for bi in range(bsz):
        h = x[bi]
        for i in range(DEPTH):
            kind, j = i % N_MIXERS, i // N_MIXERS
            g0, b0 = ln_g[i, 0].reshape(1, d), ln_b[i, 0].reshape(1, d)
            ffn = (ffn_w1[i].astype(BF16), ffn_w3[i].astype(BF16), ffn_w2[i].astype(BF16),
                   ln_g[i, 1].reshape(1, d), ln_b[i, 1].reshape(1, d))
            if kind == 0:
                h = _fox_layer(h, fox_w_qkv[j], fox_w_f[j], fox_b_f[j], fox_w_o[j], g0, b0, ffn)
            elif kind == 1:
                h = _s5_layer(h, s5_w_in[j], s5_a_re[j], s5_a_im[j], s5_log_dt[j],
                              s5_b_re[j], s5_b_im[j], s5_c_re[j], s5_c_im[j],
                              s5_d[j], s5_w_glu[j], s5_b_glu[j], g0, b0)
                h = _ffn_ln(h, ffn)
            else:
                h = _pool_layer(h, pool_w_in[j], pool_w_grp[j], pool_scale[j], pool_w_out[j],
                                g0, b0, ffn)
        outs.append(h)
    return jnp.stack(outs, axis=0)
```

```python
import math

import jax
import jax.numpy as jnp
import numpy as np
from jax import lax
from jax.experimental import pallas as pl
from jax.experimental.pallas import tpu as pltpu

F32 = jnp.float32
BF16 = jnp.bfloat16

D_MODEL = 1024
SEQ = 16384
DEPTH = 4
N_MIXERS = 3
FOX_HEADS = 16
FOX_HEAD_DIM = D_MODEL // FOX_HEADS
S5_GROUP = 16
S5_GROUPS = D_MODEL // S5_GROUP
S5_STATE = 64
POOL_WINDOWS = (2, 4, 8, 16)
POOL_GROUP = D_MODEL // len(POOL_WINDOWS)
D_FF = 2816
DN_ALPHA = (2.0 * DEPTH) ** 0.25
LN_EPS = 1e-5

LANES = 128
VMEM_LIMIT = 56 * 1024 * 1024

ROW_BLOCK = 512
FF_CHUNK = 256
ATT_BLOCK = 1024
KEY_CHUNK = 256
HEAD_SLOT = 2 * FOX_HEAD_DIM
V_SLOT = 80
LOG2E = math.log2(math.e)
S5_ROWS = 256
S5_SKEW = 7
S5_HALO = 8
S5_PITCH = S5_ROWS + S5_HALO
S5_SLABS = 64
S5_GB = 8
S5_GB_CH = S5_GB * S5_GROUP
S5_GB_ST = S5_GB * S5_STATE
POOL_HALO = 16
NEG_BIG = -1e30


def _resident(shape):
    nd = len(shape)
    return pl.BlockSpec(shape, lambda *_: (0,) * nd, pipeline_mode=pl.Buffered(1))


def _params(*sem):
    return pltpu.CompilerParams(dimension_semantics=sem, vmem_limit_bytes=VMEM_LIMIT)


def _dot(a, b):
    return jnp.dot(a, b, preferred_element_type=F32)


def _layer_norm(z, g, b):
    mu = jnp.mean(z, axis=-1, keepdims=True)
    zc = z - mu
    var = jnp.mean(zc * zc, axis=-1, keepdims=True)
    return zc * lax.rsqrt(var + LN_EPS) * g + b


def _split3(v):
    p0 = v.astype(BF16)
    r = v - p0.astype(F32)
    p1 = r.astype(BF16)
    p2 = (r - p1.astype(F32)).astype(BF16)
    return p0, p1, p2


def _ffn_block(x, w1_ref, w3_ref, w2_ref, g_ref, b_ref):
    xb = x.astype(BF16)
    acc = jnp.zeros(x.shape, F32)
    for c in range(D_FF // FF_CHUNK):
        cols = slice(c * FF_CHUNK, (c + 1) * FF_CHUNK)
        a = _dot(xb, w1_ref[:, cols])
        b = _dot(xb, w3_ref[:, cols])
        gate = (a * jax.nn.sigmoid(a) * b).astype(BF16)
        acc = acc + _dot(gate, w2_ref[cols, :])
    return _layer_norm(DN_ALPHA * x + acc, g_ref[...], b_ref[...])


def _ffn_specs(d):
    return [_resident((d, D_FF)), _resident((d, D_FF)), _resident((D_FF, d)),
            _resident((1, d)), _resident((1, d))]


def _ffn_ln_kernel(x_ref, w1_ref, w3_ref, w2_ref, g_ref, b_ref, o_ref):
    o_ref[...] = _ffn_block(x_ref[...], w1_ref, w3_ref, w2_ref, g_ref, b_ref)


def _ffn_ln(x, ffn):
    s, d = x.shape
    row = pl.BlockSpec((ROW_BLOCK, d), lambda i: (i, 0))
    return pl.pallas_call(
        _ffn_ln_kernel,
        grid=(s // ROW_BLOCK,),
        in_specs=[row] + _ffn_specs(d),
        out_specs=row,
        out_shape=jax.ShapeDtypeStruct((s, d), F32),
        compiler_params=_params("parallel"),
        name="ffn_ln",
    )(x, *ffn)


def _proj_ffn_kernel(x_ref, y_ref, w_ref, g_ref, b_ref, w1_ref, w3_ref, w2_ref, g1_ref, b1_ref,
                     o_ref):
    m = _dot(y_ref[...], w_ref[...])
    h = _layer_norm(DN_ALPHA * x_ref[...] + m, g_ref[...], b_ref[...])
    o_ref[...] = _ffn_block(h, w1_ref, w3_ref, w2_ref, g1_ref, b1_ref)


def _proj_ffn(x, y, w, g, b, ffn):
    s, d = x.shape
    row = pl.BlockSpec((ROW_BLOCK, d), lambda i: (i, 0))
    return pl.pallas_call(
        _proj_ffn_kernel,
        grid=(s // ROW_BLOCK,),
        in_specs=[row, row, _resident((d, d)), _resident((1, d)), _resident((1, d))]
        + _ffn_specs(d),
        out_specs=row,
        out_shape=jax.ShapeDtypeStruct((s, d), F32),
        compiler_params=_params("parallel"),
        name="proj_ffn",
    )(x, y, w, g, b, *ffn)


def _fox_qkv_kernel(x_ref, wqt_ref, eqt_ref, oneq_ref, wk_ref, ek_ref, onek_ref, wvt_ref, vone_ref,
                    wf_ref, bf_ref, tri_ref, qt_ref, k_ref, vt_ref, carry_ref):
    @pl.when(pl.program_id(0) == 0)
    def _():
        carry_ref[...] = jnp.zeros_like(carry_ref)

    nt = (((1,), (1,)), ((), ()))
    xb = x_ref[...].astype(BF16)
    z = _dot(xb, wf_ref[...]) + bf_ref[...]
    log_f = jnp.minimum(z, 0.0) - jnp.log1p(jnp.exp(-jnp.abs(z)))
    tri = tri_ref[...]
    p0, p1, p2 = _split3(log_f)
    c = _dot(tri, p0) + _dot(tri, p1) + _dot(tri, p2) + carry_ref[...]
    carry_ref[...] = c[ROW_BLOCK - 1:ROW_BLOCK, :]
    pieces = [p.astype(F32) for p in _split3(c * LOG2E)]
    lane = lax.broadcasted_iota(jnp.int32, c.shape, 1)
    c3 = jnp.zeros(c.shape, F32)
    for i in reversed(range(3)):
        shifted = pltpu.roll(pieces[i], i * FOX_HEADS, 1) if i else pieces[i]
        c3 = jnp.where(lane < (i + 1) * FOX_HEADS, shifted, c3)
    c3 = c3.astype(BF16)
    dh = FOX_HEAD_DIM
    k_main = _dot(xb, wk_ref[...])
    k_aug = _dot(c3, ek_ref[...]) + onek_ref[...]
    qt_main = lax.dot_general(wqt_ref[...], xb, nt, preferred_element_type=F32)
    qt_aug = lax.dot_general(eqt_ref[...], c3, nt, preferred_element_type=F32) + oneq_ref[...]
    low = lax.broadcasted_iota(jnp.int32, (ROW_BLOCK, HEAD_SLOT), 1) < dh
    for h in range(FOX_HEADS):
        slot = slice(h * HEAD_SLOT, (h + 1) * HEAD_SLOT)
        pair = k_main[:, (h // 2) * HEAD_SLOT:(h // 2 + 1) * HEAD_SLOT]
        mine = low if h % 2 == 0 else jnp.logical_not(low)
        k_ref[:, slot] = (jnp.where(mine, pair, 0.0) + k_aug[:, slot]).astype(BF16)
        main_rows = slice(h * HEAD_SLOT + (h % 2) * dh, h * HEAD_SLOT + (h % 2) * dh + dh)
        aug_rows = slice(h * HEAD_SLOT + (1 - h % 2) * dh, h * HEAD_SLOT + (1 - h % 2) * dh + dh)
        qt_ref[0, main_rows, :] = (qt_main[h * dh:(h + 1) * dh] + qt_aug[main_rows]).astype(BF16)
        qt_ref[0, aug_rows, :] = qt_aug[aug_rows].astype(BF16)
    vt = lax.dot_general(wvt_ref[...], xb, nt, preferred_element_type=F32)
    vt_ref[0] = (vt + vone_ref[...]).astype(BF16)


def _fox_qkv(x, w_qkv, w_f, b_f):
    s, d = x.shape
    nq = FOX_HEADS * HEAD_SLOT
    nv = FOX_HEADS * V_SLOT
    dh = FOX_HEAD_DIM
    scale = dh ** -0.5 * LOG2E

    wqt = (w_qkv[:, :d] * scale).T.astype(BF16)
    wk = w_qkv[:, d:2 * d].astype(BF16)
    wv = w_qkv[:, 2 * d:].T.reshape(FOX_HEADS, dh, d)
    wvt = jnp.pad(wv, ((0, 0), (0, V_SLOT - dh), (0, 0))).reshape(nv, d).astype(BF16)
    vone = np.zeros((FOX_HEADS, V_SLOT), np.float32)
    vone[:, dh] = 1.0
    vone = vone.reshape(nv, 1)
    wf =jnp.pad(w_f, ((0, 0), (0, LANES - FOX_HEADS))).astype(BF16)
    bf = jnp.pad(b_f, (0, LANES - FOX_HEADS)).reshape(1, LANES)

    heads = np.arange(FOX_HEADS)
    aug = heads * HEAD_SLOT + np.where(heads % 2 == 0, dh, 0)
    eq = np.zeros((LANES, nq), np.float32)
    ek = np.zeros((LANES, nq), np.float32)
    oneq = np.zeros((nq,), np.float32)
    onek = np.zeros((nq,), np.float32)
    for i in range(3):
        eq[i * FOX_HEADS + heads, aug + i] = 1.0
        ek[i * FOX_HEADS + heads, aug + 3 + i] = -1.0
        oneq[aug + 3 + i] = 1.0
        onek[aug + i] = 1.0
    eqt = eq.T.astype(BF16)
    ek = ek.astype(BF16)
    r = np.arange(ROW_BLOCK)
    tri = (r[:, None] >= r[None, :]).astype(BF16)

    nb = s // ROW_BLOCK
    return pl.pallas_call(
        _fox_qkv_kernel,
        grid=(nb,),
        in_specs=[pl.BlockSpec((ROW_BLOCK, d), lambda i: (i, 0)),
                  _resident((d, d)), _resident((nq, LANES)), _resident((nq, 1)),
                  _resident((d, d)), _resident((LANES, nq)), _resident((1, nq)),
                  _resident((nv, d)), _resident((nv, 1)),
                  _resident((d, LANES)), _resident((1, LANES)),
                  _resident((ROW_BLOCK, ROW_BLOCK))],
        out_specs=[pl.BlockSpec((1, nq, ROW_BLOCK), lambda i: (i, 0, 0)),
                   pl.BlockSpec((ROW_BLOCK, nq), lambda i: (i, 0)),
                   pl.BlockSpec((1, nv, ROW_BLOCK), lambda i: (i, 0, 0))],
        out_shape=[jax.ShapeDtypeStruct((nb, nq, ROW_BLOCK), BF16),
                   jax.ShapeDtypeStruct((s, nq), BF16),
                   jax.ShapeDtypeStruct((nb, nv, ROW_BLOCK), BF16)],
        scratch_shapes=[pltpu.VMEM((1, LANES), F32)],
        compiler_params=_params("arbitrary"),
        name="fox_qkv",
    )(x, wqt, eqt, oneq.reshape(nq, 1), wk, ek, onek.reshape(1, nq), wvt, vone, wf, bf, tri)


def _fox_attn_kernel(qt_ref, k_ref, vt_ref, o_ref, m_ref, acc_ref, s0_ref, s1_ref, bm_ref):
    i = pl.program_id(1)
    t = ATT_BLOCK
    rb = ROW_BLOCK
    chunks = t // KEY_CHUNK
    s_refs = (s0_ref, s1_ref)
    m_ref[...] = jnp.full(m_ref.shape, NEG_BIG, F32)
    acc_ref[...] = jnp.zeros(acc_ref.shape, F32)

    def score_chunk(h, tile, c, q0=0):
        rows = pl.ds(pl.multiple_of(tile * t + c * KEY_CHUNK, KEY_CHUNK), KEY_CHUNK)
        lanes = slice(h * HEAD_SLOT, (h + 1) * HEAD_SLOT)
        k = k_ref[rows, lanes]
        parts = [_dot(k, qt_ref[b, lanes, max(q0 - b * rb, 0):])
                 for b in range(t // rb) if q0 < (b + 1) * rb]
        st = parts[0] if len(parts) == 1 else jnp.concatenate(parts, axis=1)
        s_refs[h][c * KEY_CHUNK:(c + 1) * KEY_CHUNK, q0:] = st
        bm = jnp.max(st, axis=0, keepdims=True)
        if q0:
            bm = jnp.concatenate([jnp.full((1, q0), NEG_BIG, F32), bm], axis=1)
        bm_ref[h, c] = bm

    def value_chunk(h, tile, c, m_new, q0=0):
        p = jnp.exp2(s_refs[h][c * KEY_CHUNK:(c + 1) * KEY_CHUNK, q0:] - m_new[:, q0:]).astype(BF16)
        blk, off = divmod(c * KEY_CHUNK, rb)
        vt = vt_ref[tile * (t // rb) + blk, h * V_SLOT:(h + 1) * V_SLOT, off:off + KEY_CHUNK]
        pv = _dot(vt, p)
        if q0:
            pv = jnp.concatenate([jnp.zeros((V_SLOT, q0), F32), pv], axis=1)
        return pv

    def step(h_acc, tile_acc, h_sc, tile_sc, diagonal=False):
        m_old = m_ref[h_acc]
        m_new = m_old
        for c in range(chunks):
            m_new = jnp.maximum(m_new, bm_ref[h_acc, c])
        pv = None
        for c in range(chunks):
            q0 = c * KEY_CHUNK if diagonal else 0
            if h_sc is not None:
                score_chunk(h_sc, tile_sc, c, q0)
            contrib = value_chunk(h_acc, tile_acc, c, m_new, q0)
            pv = contrib if pv is None else pv + contrib
        acc_ref[h_acc] = jnp.exp2(m_old - m_new) * acc_ref[h_acc] + pv
        m_ref[h_acc] = m_new

    def mask_tile(h):
        kc = KEY_CHUNK
        tri = (lax.broadcasted_iota(jnp.int32, (kc, kc), 0)
               <= lax.broadcasted_iota(jnp.int32, (kc, kc), 1))
        for c in range(chunks):
            rows = slice(c * kc, (c + 1) * kc)
            d = jnp.where(tri, s_refs[h][rows, rows], NEG_BIG)
            s_refs[h][rows, rows] = d
            pieces = [jnp.max(d, axis=0, keepdims=True)]
            if c:
                pieces.insert(0, jnp.full((1, c * kc), NEG_BIG, F32))
            if c + 1 < chunks:
                pieces.append(bm_ref[h, c][:, (c + 1) * kc:])
            bm_ref[h, c] = jnp.concatenate(pieces, axis=1)

    for c in range(chunks):
        score_chunk(0, 0, c)

    def body(tile, carry):
        step(0, tile, 1, tile)
        step(1, tile, 0, tile + 1)
        return carry

    lax.fori_loop(0, i, body, 0)
    mask_tile(0)
    step(0, i, 1, i, diagonal=True)
    mask_tile(1)
    step(1, i, None, None, diagonal=True)

    outs = []
    for h in range(2):
        acc = acc_ref[h]
        outs.append(acc[:FOX_HEAD_DIM] / acc[FOX_HEAD_DIM:FOX_HEAD_DIM + 1])
    o_ref[...] = jnp.concatenate(outs, axis=0).T.astype(o_ref.dtype)


def _fox_attn(qt, k, vt):
    s = k.shape[0]
    t = ATT_BLOCK
    pair = 2 * HEAD_SLOT
    return pl.pallas_call(
        _fox_attn_kernel,
        grid=(FOX_HEADS // 2, s // t),
        in_specs=[pl.BlockSpec((t // ROW_BLOCK, pair, ROW_BLOCK), lambda hp, i: (i, hp, 0)),
                  pl.BlockSpec((s, pair), lambda hp, i: (0, hp)),
                  pl.BlockSpec((s // ROW_BLOCK, 2 * V_SLOT, ROW_BLOCK), lambda hp, i: (0, hp, 0))],
        out_specs=pl.BlockSpec((t, LANES), lambda hp, i: (i, hp)),
        out_shape=jax.ShapeDtypeStruct((s, D_MODEL), BF16),
        scratch_shapes=[pltpu.VMEM((2, 1, t), F32), pltpu.VMEM((2, V_SLOT, t), F32),
                        pltpu.VMEM((t, t), F32), pltpu.VMEM((t, t), F32),
                        pltpu.VMEM((2, t // KEY_CHUNK, 1, t), F32)],
        compiler_params=_params("parallel", "arbitrary"),
        name="fox_attn",
    )(qt, k, vt)


def _fox_layer(h, w_qkv, w_f, b_f, w_o, g, b, ffn):
    qt, k, vt = _fox_qkv(h, w_qkv, w_f, b_f)
    o = _fox_attn(qt, k, vt)
    return _proj_ffn(h, o, w_o.astype(BF16), g, b, ffn)


def _s5_slab(gb, part, c):
    return (gb // 2) * 16 + part * 8 + (gb % 2) * 4 + c


def _s5_kernel(x_ref, win_ref, bm_ref, cm_ref, lam_ref, d_ref, wglu_ref, bglu_ref, g_ref, b_ref,
               o_ref, bu_ref, h_ref, state_ref, y_ref):
    n = S5_ROWS
    pitch = S5_PITCH
    stride = pitch - 1
    ngb = S5_GROUPS // S5_GB
    nvg = S5_SLABS // 16

    @pl.when(pl.program_id(0) == 0)
    def _():
        bu_ref[...] = jnp.zeros_like(bu_ref)
        state_ref[...] = jnp.zeros_like(state_ref)

    x = x_ref[...]
    u = _dot(x.astype(BF16), win_ref[...])
    for gb in range(ngb):
        bu = _dot(u[:, gb * S5_GB_CH:(gb + 1) * S5_GB_CH].astype(BF16), bm_ref[gb])
        for c in range(8):
            slab = _s5_slab(gb, c // 4, c % 4)
            bu_ref[pl.ds(slab * pitch + S5_HALO, n), :] = bu[:, c * LANES:(c + 1) * LANES]

    lam = [(lam_ref[0, v], lam_ref[1, v]) for v in range(nvg)]
    sub = lax.broadcasted_iota(jnp.int32, (8, LANES), 0)

    def step(k, state, valid):
        new_state = []
        for v in range(nvg):
            lr, li = lam[v]
            hr, hi = state[2 * v], state[2 * v + 1]
            rows_r = pl.ds(v * 16 * pitch + S5_HALO + k, 8, stride=stride)
            rows_i = pl.ds((v * 16 + 8) * pitch + S5_HALO + k, 8, stride=stride)
            nr = lr * hr - li * hi + bu_ref[rows_r, :]
            ni = lr * hi + li * hr + bu_ref[rows_i, :]
            if valid is not None:
                nr = jnp.where(valid, nr, hr)
                ni = jnp.where(valid, ni, hi)
            h_ref[rows_r, :] = nr
            h_ref[rows_i, :] = ni
            new_state += [nr, ni]
        return tuple(new_state)

    state = tuple(state_ref[i] for i in range(2 * nvg))
    for k in range(S5_SKEW):
        state = step(k, state, sub <= k)
    state = lax.fori_loop(S5_SKEW, n, lambda k, st: step(k, st, None), state, unroll=4)
    for k in range(n, n + S5_SKEW):
        state = step(k, state, sub > k - n)
    for i in range(2 * nvg):
        state_ref[i] = state[i]

    for gb in range(ngb):
        ch = slice(gb * S5_GB_CH, (gb + 1) * S5_GB_CH)
        parts = [h_ref[pl.ds(_s5_slab(gb, c // 4, c % 4) * pitch + S5_HALO, n), :]
                 for c in range(8)]
        hcat = jnp.concatenate(parts, axis=1).astype(BF16)
        y = _dot(hcat, cm_ref[gb]) + d_ref[:, ch] * u[:, ch]
        y_ref[:, ch] = jax.nn.gelu(y).astype(BF16)

    d = x.shape[1]
    z = _dot(y_ref[...], wglu_ref[...]) + bglu_ref[...]
    m = z[:, :d] * jax.nn.sigmoid(z[:, d:])
    o_ref[...] = _layer_norm(DN_ALPHA * x + m, g_ref[...], b_ref[...])


def _s5_tables(a_re, a_im, log_dt, b_re, b_im, c_re, c_im):
    dt = jnp.exp(log_dt)[:, None]
    mag = jnp.exp(a_re * dt)
    lr, li = mag * jnp.cos(a_im * dt), mag * jnp.sin(a_im * dt)
    den = a_re * a_re + a_im * a_im
    zr = ((lr - 1.0) * a_re + li * a_im) / den
    zi = (li * a_re - (lr - 1.0) * a_im) / den
    bbr = zr[..., None] * b_re - zi[..., None] * b_im
    bbi = zr[..., None] * b_im + zi[..., None] * b_re

    ngb = S5_GROUPS // S5_GB
    eye = np.eye(S5_GB, dtype=np.float32)

    def blockdiag_in(bb):
        t = bb.reshape(ngb, S5_GB, S5_STATE, S5_GROUP).transpose(0, 1, 3, 2)
        t = t[:, :, :, None, :] * eye[None, :, None, :, None]
        return t.reshape(ngb, S5_GB_CH, S5_GB_ST)

    def blockdiag_out(cc):
        t = cc.reshape(ngb, S5_GB, S5_GROUP, S5_STATE).transpose(0, 1, 3, 2)
        t = t[:, :, :, None, :] * eye[None, :, None, :, None]
        return t.reshape(ngb, S5_GB_ST, S5_GB_CH)

    bm = jnp.concatenate([blockdiag_in(bbr), blockdiag_in(bbi)], axis=2).astype(BF16)
    cm = jnp.concatenate([blockdiag_out(c_re), blockdiag_out(-c_im)], axis=1).astype(BF16)

    lam = jnp.stack([lr, li]).reshape(2, S5_SLABS // 16, 8, LANES)
    return bm, cm, lam


def _s5_layer(h, w_in, a_re, a_im, log_dt, b_re, b_im, c_re, c_im, d_skip, w_glu, b_glu, g, b):
    s, d = h.shape
    bm, cm, lam = _s5_tables(a_re, a_im, log_dt, b_re, b_im, c_re, c_im)
    n = S5_ROWS
    row = pl.BlockSpec((n, d), lambda i: (i, 0))
    slab_rows = S5_SLABS * S5_PITCH + S5_HALO
    return pl.pallas_call(
        _s5_kernel,
        grid=(s // n,),
        in_specs=[row, _resident((d, d)), _resident(bm.shape), _resident(cm.shape),
                  _resident(lam.shape), _resident((1, d)), _resident((d, 2 * d)),
                  _resident((1, 2 * d)), _resident((1, d)), _resident((1, d))],
        out_specs=row,
        out_shape=jax.ShapeDtypeStruct((s, d), F32),
        scratch_shapes=[pltpu.VMEM((slab_rows, LANES), F32), pltpu.VMEM((slab_rows, LANES), F32),
                        pltpu.VMEM((S5_SLABS // 8, 8, LANES), F32), pltpu.VMEM((n, d), BF16)],
        compiler_params=_params("arbitrary"),
        name="s5",
    )(h, w_in.astype(BF16), bm, cm, lam, d_skip.reshape(1, d), w_glu.astype(BF16),
      b_glu.reshape(1, 2 * d), g, b)


def _pool_ffn_kernel(x_ref, win_ref, wgrp_ref, scale_ref, wout_ref, g_ref, b_ref,
                     w1_ref, w3_ref, w2_ref, g1_ref, b1_ref, o_ref, ext_ref):
    i = pl.program_id(0)
    n = ROW_BLOCK

    @pl.when(i == 0)
    def _():
        ext_ref[0:POOL_HALO, :] = jnp.zeros((POOL_HALO, D_MODEL), F32)

    x = x_ref[...]
    u = _dot(x.astype(BF16), win_ref[...])
    ext_ref[POOL_HALO:, :] = u
    acc = ext_ref[...]
    pos1 = (i * n + 1 + lax.broadcasted_iota(jnp.int32, (n, 1), 0)).astype(F32)
    mixed = []
    for gi, w in enumerate(POOL_WINDOWS):
        acc = acc[:, (POOL_GROUP if gi else 0):]
        acc = acc + pltpu.roll(acc, w // 2, 0)
        cols = slice(gi * POOL_GROUP, (gi + 1) * POOL_GROUP)
        mean = acc[POOL_HALO:, :POOL_GROUP] / jnp.minimum(pos1, float(w))
        m = (mean - u[:, cols]).astype(BF16)
        mixed.append(_dot(m, wgrp_ref[gi]))
    m = (jnp.concatenate(mixed, axis=1) * scale_ref[...]).astype(BF16)
    y = _dot(m, wout_ref[...])
    ext_ref[0:POOL_HALO, :] = u[n - POOL_HALO:, :]
    h = _layer_norm(DN_ALPHA * x + y, g_ref[...], b_ref[...])
    o_ref[...] = _ffn_block(h, w1_ref, w3_ref, w2_ref, g1_ref, b1_ref)


def _pool_layer(h, w_in, w_grp, scale, w_out, g, b, ffn):
    s, d = h.shape
    row = pl.BlockSpec((ROW_BLOCK, d), lambda i: (i, 0))
    return pl.pallas_call(
        _pool_ffn_kernel,
        grid=(s // ROW_BLOCK,),
        in_specs=[row, _resident((d, d)), _resident(w_grp.shape), _resident((1, d)),
                  _resident((d, d)), _resident((1, d)), _resident((1, d))] + _ffn_specs(d),
        out_specs=row,
        out_shape=jax.ShapeDtypeStruct((s, d), F32),
        scratch_shapes=[pltpu.VMEM((ROW_BLOCK + POOL_HALO, d), F32)],
        compiler_params=_params("arbitrary"),
        name="pool_ffn",
    )(h, w_in.astype(BF16), w_grp.astype(BF16), scale.reshape(1, d), w_out.astype(BF16), g, b,
      *ffn)


def kernel(x, ln_g, ln_b, fox_w_qkv, fox_w_f, fox_b_f, fox_w_o, s5_w_in, s5_a_re, s5_a_im, s5_log_dt, s5_b_re, s5_b_im, s5_c_re, s5_c_im, s5_d, s5_w_glu, s5_b_glu, pool_w_in, pool_w_grp, pool_scale, pool_w_out, ffn_w1, ffn_w3, ffn_w2):
    bsz, s, d = x.shape
    outs = []
    for bi in range(bsz):
        h = x[bi]
        for i in range(DEPTH):
            kind, j = i % N_MIXERS, i // N_MIXERS
            g0, b0 = ln_g[i, 0].reshape(1, d), ln_b[i, 0].reshape(1, d)
            ffn = (ffn_w1[i].astype(BF16), ffn_w3[i].astype(BF16), ffn_w2[i].astype(BF16),
                   ln_g[i, 1].reshape(1, d), ln_b[i, 1].reshape(1, d))
            if kind == 0:
                h = _fox_layer(h, fox_w_qkv[j], fox_w_f[j], fox_b_f[j], fox_w_o[j], g0, b0, ffn)
            elif kind == 1:
                h = _s5_layer(h, s5_w_in[j], s5_a_re[j], s5_a_im[j], s5_log_dt[j],
                              s5_b_re[j], s5_b_im[j], s5_c_re[j], s5_c_im[j],
                              s5_d[j], s5_w_glu[j], s5_b_glu[j], g0, b0)
                h = _ffn_ln(h, ffn)
            else:
                h = _pool_layer(h, pool_w_in[j], pool_w_grp[j], pool_scale[j], pool_w_out[j],
                                g0, b0, ffn)
        outs.append(h)
    return jnp.stack(outs, axis=0)
```
